```python
import jax, jax.numpy as jnp
from jax import lax
import numpy as np

D_MODEL = 1024
BATCH = 2
SEQ = 8192
DEPTH = 2

N_MOD = 6
CONV_CH = 512
CONV_WIDTH = 31
N_HEADS = 8
HEAD_DIM = 64
ATTN_W = N_HEADS * HEAD_DIM
IDX_HEADS = 8
IDX_DIM = 32
TOPK_MAX = 256
Q_BLOCK = 128
ROPE_THETA = 10000.0
INDEX_SCALE = (IDX_DIM ** -0.5) * (IDX_HEADS ** -0.5)
N_BRANCH = 2
N_GROUPS = 4
EXPERTS_PER_GROUP = 8
N_EXPERTS = N_GROUPS * EXPERTS_PER_GROUP
TOP_K_INNER = 2
D_EXPERT = 256
EPS = 1e-6

IN_SIZES = (CONV_CH, CONV_CH, ATTN_W, HEAD_DIM, HEAD_DIM, IDX_HEADS * IDX_DIM, IDX_DIM, IDX_HEADS, N_BRANCH * D_MODEL)
N_IN = 2 * CONV_CH + ATTN_W + 2 * HEAD_DIM + IDX_HEADS * IDX_DIM + IDX_DIM + IDX_HEADS + N_BRANCH * D_MODEL

kernel_name = "hybrid_conformer_dsa_hmoe_adaln"


def rms_norm(x, g):
    x32 = x.astype(jnp.float32)
    y = x32 * lax.rsqrt(jnp.mean(x32 * x32, axis=-1, keepdims=True) + EPS)
    return (y * g.astype(jnp.float32)).astype(x.dtype)


def layer_norm(x, g, b):
    x32 = x.astype(jnp.float32)
    mu = jnp.mean(x32, axis=-1, keepdims=True)
    xc = x32 - mu
    y = xc * lax.rsqrt(jnp.mean(xc * xc, axis=-1, keepdims=True) + EPS)
    return (y * g.astype(jnp.float32) + b.astype(jnp.float32)).astype(x.dtype)


def rope_tables(length, dim):
    pos = jnp.arange(length, dtype=jnp.float32)
    inv = ROPE_THETA ** (-jnp.arange(0, dim, 2, dtype=jnp.float32) / dim)
    ang = pos[:, None] * inv[None, :]
    return jnp.cos(ang), jnp.sin(ang)


def apply_rope(x, cos, sin):
    L = x.shape[1]
    half = x.shape[-1] // 2
    shp = (1, L) + (1,) * (x.ndim - 3) + (half,)
    cos = cos.reshape(shp)
    sin = sin.reshape(shp)
    x1 = x[..., :half].astype(jnp.float32)
    x2 = x[..., half:].astype(jnp.float32)
    out = jnp.concatenate([x1 * cos - x2 * sin, x2 * cos + x1 * sin], axis=-1)
    return out.astype(x.dtype)


def conformer_conv(a, b, conv_w, conv_b, ln_g, ln_b, w_o, b_o):
    u = a * jax.nn.sigmoid(b)
    y = lax.conv_general_dilated(
        u, conv_w[:, None, :], window_strides=(1,),
        padding=[(CONV_WIDTH - 1, 0)],
        dimension_numbers=("NWC", "WIO", "NWC"),
        feature_group_count=CONV_CH) + conv_b
    y = jax.nn.silu(layer_norm(y, ln_g, ln_b))
    return y @ w_o + b_o


def dsa_attention(q, k, v, iq, ik, iw, w_o):
    B, L = q.shape[0], q.shape[1]
    k_sel = min(TOPK_MAX, L // 4)
    qb = min(Q_BLOCK, L)
    n_blocks = L // qb
    scale = HEAD_DIM ** -0.5
    key_pos = jnp.arange(L)

    def block(i):
        start = i * qb
        q_blk = lax.dynamic_slice_in_dim(q, start, qb, axis=1)
        iq_blk = lax.dynamic_slice_in_dim(iq, start, qb, axis=1)
        iw_blk = lax.dynamic_slice_in_dim(iw, start, qb, axis=1)
        t = start + jnp.arange(qb)
        rel = jax.nn.relu(jnp.einsum("bqhd,bsd->bqhs", iq_blk, ik).astype(jnp.float32))
        score = jnp.einsum("bqhs,bqh->bqs", rel, iw_blk.astype(jnp.float32)) * INDEX_SCALE
        causal = key_pos[None, :] <= t[:, None]
        score = jnp.where(causal[None], score, -jnp.inf)
        _, idx = lax.top_k(score, k_sel)
        valid = idx <= t[None, :, None]
        k_g = jax.vmap(lambda kk, ii: kk[ii])(k, idx)
        v_g = jax.vmap(lambda vv, ii: vv[ii])(v, idx)
        s = jnp.einsum("bqhd,bqkd->bqhk", q_blk, k_g).astype(jnp.float32) * scale
        s = jnp.where(valid[:, :, None, :], s, -jnp.inf)
        p = jax.nn.softmax(s, axis=-1).astype(v.dtype)
        return jnp.einsum("bqhk,bqkd->bqhd", p, v_g)

    out = lax.map(block, jnp.arange(n_blocks))
    out = jnp.transpose(out, (1, 0, 2, 3, 4)).reshape(B, L, ATTN_W)
    return out @ w_o


def hierarchical_moe(h, w_group, b_group, w_router, b_router, w_gate, w_up, w_down):
    B, L, D = h.shape
    t = h.reshape(B * L, D)
    T = B * L
    g_logits = (t @ w_group + b_group).astype(jnp.float32)
    g_prob = jax.nn.softmax(g_logits, axis=-1)
    g_sel = jnp.argmax(g_logits, axis=-1)
    p_group = jnp.take_along_axis(g_prob, g_sel[:, None], axis=-1)
    e_logits = (t @ w_router + b_router).astype(jnp.float32).reshape(T, N_GROUPS, EXPERTS_PER_GROUP)
    e_in = jnp.take_along_axis(e_logits, g_sel[:, None, None], axis=1)[:, 0]
    top_v, top_i = lax.top_k(e_in, TOP_K_INNER)
    p_in = jax.nn.softmax(top_v, axis=-1)
    expert_id = g_sel[:, None] * EXPERTS_PER_GROUP + top_i
    combine = jnp.sum(jax.nn.one_hot(expert_id, N_EXPERTS, dtype=jnp.float32)
                      * (p_group * p_in)[..., None], axis=1).astype(h.dtype)
    y = jnp.zeros_like(t)
    for e in range(N_EXPERTS):
        hid = jax.nn.silu(t @ w_gate[e]) * (t @ w_up[e])
        y = y + combine[:, e:e + 1] * (hid @ w_down[e])
    return y.reshape(B, L, D)


def setup_inputs(seed: int = 0) -> dict:
    key = jax.random.key(seed)
    ks = jax.random.split(key, 32)
    f32 = jnp.float32

    def nrm(k, shape, scale):
        return jax.random.normal(k, shape, f32) * scale

    D = D_MODEL
    return {
        "x": nrm(ks[0], (BATCH, SEQ, D), 1.0),
        "c": nrm(ks[1], (BATCH, D), 1.0),
        "w_ada": nrm(ks[2], (DEPTH, D, N_MOD * D), 0.5 * D ** -0.5),
        "b_ada": nrm(ks[3], (DEPTH, N_MOD * D), 0.02),
        "g_norm1": 1.0 + nrm(ks[4], (DEPTH, D), 0.02),
        "w_in": nrm(ks[5], (DEPTH, D, N_IN), D ** -0.5),
        "q_norm_g": 1.0 + nrm(ks[6], (DEPTH, HEAD_DIM), 0.02),
        "k_norm_g": 1.0 + nrm(ks[7], (DEPTH, HEAD_DIM), 0.02),
        "conv_w": nrm(ks[8], (DEPTH, CONV_WIDTH, CONV_CH), CONV_WIDTH ** -0.5),
        "conv_b": nrm(ks[9], (DEPTH, CONV_CH), 0.02),
        "conv_ln_g": 1.0 + nrm(ks[10], (DEPTH, CONV_CH), 0.02),
        "conv_ln_b": nrm(ks[11], (DEPTH, CONV_CH), 0.02),
        "w_conv_out": nrm(ks[12], (DEPTH, CONV_CH, D), CONV_CH ** -0.5),
        "b_conv_out": nrm(ks[13], (DEPTH, D), 0.02),
        "w_attn_out": nrm(ks[14], (DEPTH, ATTN_W, D), ATTN_W ** -0.5),
        "w_out": nrm(ks[15], (DEPTH, D, D), D ** -0.5),
        "g_norm2": 1.0 + nrm(ks[16], (DEPTH, D), 0.02),
        "w_group": nrm(ks[17], (DEPTH, D, N_GROUPS), D ** -0.5),
        "b_group": nrm(ks[18], (DEPTH, N_GROUPS), 0.01),
        "w_router": nrm(ks[19], (DEPTH, D, N_EXPERTS), D ** -0.5),
        "b_router": nrm(ks[20], (DEPTH, N_EXPERTS), 0.01),
        "w_gate": nrm(ks[21], (DEPTH, N_EXPERTS, D, D_EXPERT), D ** -0.5),
        "w_up": nrm(ks[22], (DEPTH, N_EXPERTS, D, D_EXPERT), D ** -0.5),
        "w_down": nrm(ks[23], (DEPTH, N_EXPERTS, D_EXPERT, D), D_EXPERT ** -0.5),
    }


def reference(x, c, w_ada, b_ada, g_norm1, w_in, q_norm_g, k_norm_g, conv_w, conv_b,
              conv_ln_g, conv_ln_b, w_conv_out, b_conv_out, w_attn_out, w_out, g_norm2,
              w_group, b_group, w_router, b_router, w_gate, w_up, w_down):
    B, L, D = x.shape
    cos_h, sin_h = rope_tables(L, HEAD_DIM)
    cos_i, sin_i = rope_tables(L, IDX_DIM)
    split_at = [int(s) for s in np.cumsum(IN_SIZES)[:-1]]
    for l in range(DEPTH):
        mod = jax.nn.silu(c) @ w_ada[l] + b_ada[l]
        shift1, scale1, gate1, shift2, scale2, gate2 = jnp.split(mod, N_MOD, axis=-1)

        h = rms_norm(x, g_norm1[l]) * (1.0 + scale1[:, None]) + shift1[:, None]
        u = h @ w_in[l]
        ca, cb, uq, uk, uv, uiq, uik, uiw, ug = jnp.split(u, split_at, axis=-1)

        y_conv = conformer_conv(ca, cb, conv_w[l], conv_b[l], conv_ln_g[l], conv_ln_b[l],
                                w_conv_out[l], b_conv_out[l])

        q = apply_rope(rms_norm(uq.reshape(B, L, N_HEADS, HEAD_DIM), q_norm_g[l]), cos_h, sin_h)
        k = apply_rope(rms_norm(uk, k_norm_g[l]), cos_h, sin_h)
        iq = apply_rope(uiq.reshape(B, L, IDX_HEADS, IDX_DIM), cos_i, sin_i)
        ik = apply_rope(uik, cos_i, sin_i)
        y_attn = dsa_attention(q, k, uv, iq, ik, uiw, w_attn_out[l])

        gates = jax.nn.sigmoid(ug.reshape(B, L, N_BRANCH, D))
        merged = (gates[:, :, 0] * y_conv + gates[:, :, 1] * y_attn) @ w_out[l]
        x = x + gate1[:, None] * merged

        h2 = rms_norm(x, g_norm2[l]) * (1.0 + scale2[:, None]) + shift2[:, None]
        x = x + gate2[:, None] * hierarchical_moe(h2, w_group[l], b_group[l], w_router[l],
                                                   b_router[l], w_gate[l], w_up[l], w_down[l])
    return x
```

```python
import functools

import numpy as np
import jax
import jax.numpy as jnp
from jax import lax
from jax.experimental import pallas as pl
from jax.experimental.pallas import tpu as pltpu

F32 = jnp.float32
BF16 = jnp.bfloat16
I32 = jnp.int32

N_MOD = 6
CONV_CH = 512
CONV_WIDTH = 31
N_HEADS = 8
HEAD_DIM = 64
ATTN_W = N_HEADS * HEAD_DIM
IDX_HEADS = 8
IDX_DIM = 32
TOPK_MAX = 256
ROPE_THETA = 10000.0
INDEX_SCALE = (IDX_DIM ** -0.5) * (IDX_HEADS ** -0.5)
N_GROUPS = 4
EXPERTS_PER_GROUP = 8
N_EXPERTS = N_GROUPS * EXPERTS_PER_GROUP
EPS = 1e-6

LANES = 128
VMEM_LIMIT = 48 * 1024 * 1024
IDX_SUB = 256
CHEAP_ITERS = 20
HALO = 32


def _cparams(sem):
    return pltpu.CompilerParams(dimension_semantics=sem, vmem_limit_bytes=VMEM_LIMIT)


def _rms_mod(x, g, scale, shift):
    ms = jnp.mean(x * x, axis=-1, keepdims=True)
    y = x * lax.rsqrt(ms + EPS) * g
    return y * (1.0 + scale) + shift


def _rot_half(x, half):
    n = x.shape[-1]
    lane = lax.broadcasted_iota(I32, x.shape, x.ndim - 1)
    first = (lane % (2 * half)) < half
    return jnp.where(first, pltpu.roll(x, n - half, x.ndim - 1), pltpu.roll(x, half, x.ndim - 1))


def _mod_kernel(c_ref, w_ref, b_ref, o_ref):
    c = c_ref[...]
    sc = (c * jax.nn.sigmoid(c)).astype(BF16)
    o_ref[...] = jnp.dot(sc, w_ref[...].astype(BF16), preferred_element_type=F32) + b_ref[...]


def _modulation(c, w_ada, b_ada):
    depth, d, n = w_ada.shape
    b = c.shape[0]
    rows = 8
    c_pad = jnp.zeros((rows, d), F32).at[:b].set(c)
    tn = 1536
    out = pl.pallas_call(
        _mod_kernel,
        grid=(depth, n // tn),
        in_specs=[
            pl.BlockSpec((rows, d), lambda l, j: (0, 0)),
            pl.BlockSpec((None, d, tn), lambda l, j: (l, 0, j)),
            pl.BlockSpec((None, 1, tn), lambda l, j: (l, 0, j)),
        ],
        out_specs=pl.BlockSpec((None, rows, tn), lambda l, j: (l, 0, j)),
        out_shape=jax.ShapeDtypeStruct((depth, rows, n), F32),
        compiler_params=_cparams(("arbitrary", "arbitrary")),
        name="adaln_mod",
    )(c_pad, w_ada, b_ada.reshape(depth, 1, n))
    return out[:, :b]


def _inproj_kernel(x_ref, sc_ref, sh_ref, g1_ref, wc_ref, wq_ref, wkv_ref, wiq_ref, wik_ref, ones_ref,
                   qg_ref, kg_ref, cq_ref, sq_ref, ckv_ref, skv_ref, ci_ref, si_ref, cik_ref, sik_ref,
                   uglu_ref, q_ref, k_ref, v_ref, iq_ref, ik_ref, iw_ref):
    hb = _rms_mod(x_ref[...], g1_ref[...], sc_ref[...], sh_ref[...]).astype(BF16)

    uc = jnp.dot(hb, wc_ref[...], preferred_element_type=F32)
    uglu_ref[...] = uc[:, :CONV_CH] * jax.nn.sigmoid(uc[:, CONV_CH:])

    uq = jnp.dot(hb, wq_ref[...], preferred_element_type=F32)
    sq = uq * uq
    sq_hi = sq.astype(BF16)
    sq_lo = (sq - sq_hi.astype(F32)).astype(BF16)
    ssq = (jnp.dot(sq_hi, ones_ref[...], preferred_element_type=F32)
           + jnp.dot(sq_lo, ones_ref[...], preferred_element_type=F32))
    qn = uq * lax.rsqrt(ssq * (1.0 / HEAD_DIM) + EPS) * qg_ref[...]
    qr = (qn * cq_ref[...] + _rot_half(qn, HEAD_DIM // 2) * sq_ref[...]) * (HEAD_DIM ** -0.5)
    for h in range(N_HEADS):
        q_ref[h] = qr[:, h * HEAD_DIM:(h + 1) * HEAD_DIM].astype(BF16)

    ukv = jnp.dot(hb, wkv_ref[...], preferred_element_type=F32)
    lane = lax.broadcasted_iota(I32, ukv.shape, 1)
    is_k = lane < HEAD_DIM
    ssk = jnp.sum(jnp.where(is_k, ukv * ukv, 0.0), axis=-1, keepdims=True)
    kn = ukv * lax.rsqrt(ssk * (1.0 / HEAD_DIM) + EPS) * kg_ref[...]
    kr = kn * ckv_ref[...] + _rot_half(kn, HEAD_DIM // 2) * skv_ref[...]
    k_ref[...] = kr[:, :HEAD_DIM].astype(BF16)
    v_ext = jnp.where(is_k, pltpu.roll(ukv, HEAD_DIM, 1), jnp.where(lane == HEAD_DIM, 1.0, 0.0))
    v_ref[...] = v_ext.astype(BF16)

    uiq = jnp.dot(hb, wiq_ref[...], preferred_element_type=F32)
    iqr = uiq * ci_ref[...] + _rot_half(uiq, IDX_DIM // 2) * si_ref[...]
    for h in range(IDX_HEADS):
        iq_ref[h] = iqr[:, h * IDX_DIM:(h + 1) * IDX_DIM].astype(BF16)

    uik = jnp.dot(hb, wik_ref[...], preferred_element_type=F32)
    ikr = uik * cik_ref[...] + _rot_half(uik, IDX_DIM // 2) * sik_ref[...]
    ik_ref[...] = ikr[:, :IDX_DIM].astype(BF16)
    iw_ref[...] = ikr[:, IDX_DIM:IDX_DIM + IDX_HEADS] * INDEX_SCALE


def _in_projection(x, scale1, shift1, g1, wts, tabs, tm):
    b, l, d = x.shape
    wc, wq, wkv, wiq, wik, ones_bd, qg, kg = wts
    grid = (b, l // tm)
    full = lambda a: pl.BlockSpec(a.shape, lambda bi, i: (0,) * a.ndim)
    tok = lambda n: pl.BlockSpec((None, tm, n), lambda bi, i: (bi, i, 0))
    vec = pl.BlockSpec((None, 1, d), lambda bi, i: (bi, 0, 0))
    tab = lambda a: pl.BlockSpec((tm, a.shape[1]), lambda bi, i: (i, 0))
    head = lambda n: pl.BlockSpec((None, N_HEADS, tm, n), lambda bi, i: (bi, 0, i, 0))
    return pl.pallas_call(
        _inproj_kernel,
        grid=grid,
        in_specs=[tok(d), vec, vec, full(g1), full(wc), full(wq), full(wkv), full(wiq), full(wik),
                  full(ones_bd), full(qg), full(kg)] + [tab(t) for t in tabs],
        out_specs=[tok(CONV_CH), head(HEAD_DIM), tok(HEAD_DIM), tok(LANES), head(IDX_DIM),
                   tok(IDX_DIM), tok(IDX_HEADS)],
        out_shape=[
            jax.ShapeDtypeStruct((b, l, CONV_CH), F32),
            jax.ShapeDtypeStruct((b, N_HEADS, l, HEAD_DIM), BF16),
            jax.ShapeDtypeStruct((b, l, HEAD_DIM), BF16),
            jax.ShapeDtypeStruct((b, l, LANES), BF16),
            jax.ShapeDtypeStruct((b, IDX_HEADS, l, IDX_DIM), BF16),
            jax.ShapeDtypeStruct((b, l, IDX_DIM), BF16),
            jax.ShapeDtypeStruct((b, l, IDX_HEADS), F32),
        ],
        compiler_params=_cparams(("arbitrary", "arbitrary")),
        name="in_projection",
    )(x, scale1, shift1, g1, wc, wq, wkv, wiq, wik, ones_bd, qg, kg, *tabs)


def _conv_kernel(u_ref, halo_ref, w_ref, cb_ref, g_ref, b_ref, o_ref, buf_ref, *, tl, rows):
    i = pl.program_id(1)
    buf_ref[0:HALO] = jnp.where(i > 0, halo_ref[...], 0.0)
    buf_ref[HALO:] = u_ref[...]
    first = HALO - (CONV_WIDTH - 1)
    for r in range(tl // rows):
        acc = jnp.zeros((rows, CONV_CH), F32) + cb_ref[...]
        for j in range(CONV_WIDTH):
            s = r * rows + first + j
            acc = acc + buf_ref[s:s + rows, :] * w_ref[j:j + 1, :]
        mu = jnp.mean(acc, axis=-1, keepdims=True)
        yc = acc - mu
        var = jnp.mean(yc * yc, axis=-1, keepdims=True)
        yn = yc * lax.rsqrt(var + EPS) * g_ref[...] + b_ref[...]
        o_ref[r * rows:(r + 1) * rows, :] = (yn * jax.nn.sigmoid(yn)).astype(BF16)


def _conv_branch(uglu, conv_w, conv_b, ln_g, ln_b, tl):
    b, l, ch = uglu.shape
    per = tl // HALO
    w_pad = jnp.zeros((HALO, ch), F32).at[:CONV_WIDTH].set(conv_w)
    row = lambda a: a.reshape(1, ch)
    full = lambda r: pl.BlockSpec((r, ch), lambda bi, i: (0, 0))
    return pl.pallas_call(
        functools.partial(_conv_kernel, tl=tl, rows=64),
        grid=(b, l // tl),
        in_specs=[
            pl.BlockSpec((None, tl, ch), lambda bi, i: (bi, i, 0)),
            pl.BlockSpec((None, HALO, ch), lambda bi, i: (bi, jnp.maximum(i * per - 1, 0), 0)),
            full(HALO), full(1), full(1), full(1),
        ],
        out_specs=pl.BlockSpec((None, tl, ch), lambda bi, i: (bi, i, 0)),
        out_shape=jax.ShapeDtypeStruct((b, l, ch), BF16),
        scratch_shapes=[pltpu.VMEM((tl + HALO, ch), F32)],
        compiler_params=_cparams(("arbitrary", "arbitrary")),
        name="conformer_conv",
    )(uglu, uglu, w_pad, row(conv_b), row(ln_g), row(ln_b))


def _dsa_kernel(q_ref, k_ref, v_ref, iq_ref, ik_ref, iw_ref, o_ref,
                sc_ref, wb_ref, bias_ref, s_ref, p_ref, m_ref, alpha_ref, acc_ref, *, tq, sk, ksel):
    i = pl.program_id(1)
    nk = ((i + 1) * tq + sk - 1) // sk
    nl = sk // LANES
    rows = N_HEADS * tq
    nt_dims = (((1,), (1,)), ((), ()))
    kf = float(ksel)
    inf = jnp.inf

    def lanes(c):
        return slice(c * LANES, (c + 1) * LANES)

    def head_rows(h):
        return slice(h * tq, (h + 1) * tq)

    iw = iw_ref[...]
    for h in range(IDX_HEADS):
        wb_ref[h] = jnp.broadcast_to(iw[:, h:h + 1], (tq, LANES))
    iq2 = iq_ref[...].reshape(IDX_HEADS * tq, IDX_DIM)
    qpos = i * tq + lax.broadcasted_iota(I32, (tq, LANES), 0)
    lane = lax.broadcasted_iota(I32, (tq, LANES), 1)

    def score_body(j, carry):
        mn, mx = carry
        base = pl.multiple_of(j * sk, sk)
        for c2 in range(sk // IDX_SUB):
            ikc = ik_ref[pl.ds(base + c2 * IDX_SUB, IDX_SUB), :]
            r = lax.dot_general(iq2, ikc, nt_dims, preferred_element_type=F32)
            for half in range(IDX_SUB // LANES):
                c = c2 * (IDX_SUB // LANES) + half
                sc = None
                for h in range(IDX_HEADS):
                    t = jnp.maximum(r[head_rows(h), lanes(half)], 0.0) * wb_ref[h]
                    sc = t if sc is None else sc + t
                mn = jnp.minimum(mn, sc)
                mx = jnp.maximum(mx, sc)
                kpos = base + c * LANES + lane
                sc_ref[j, :, lanes(c)] = jnp.where(kpos <= qpos, sc, -inf)
        return mn, mx

    mn, mx = lax.fori_loop(0, nk, score_body,
                           (jnp.full((tq, LANES), inf, F32), jnp.full((tq, LANES), -inf, F32)))

    def sweep(cand, fn, init, reduce):
        cb = jnp.broadcast_to(cand, (tq, LANES))

        def body(j, acc):
            for c in range(nl):
                acc = fn(acc, sc_ref[j, :, lanes(c)], cb)
            return acc

        return reduce(lax.fori_loop(0, nk, body, init), axis=1, keepdims=True)

    zeros = jnp.zeros((tq, LANES), F32)
    count_ge = lambda cand: sweep(cand, lambda a, s, cb: a + jnp.where(s >= cb, 1.0, 0.0), zeros, jnp.sum)
    count_gt = lambda cand: sweep(cand, lambda a, s, cb: a + jnp.where(s > cb, 1.0, 0.0), zeros, jnp.sum)
    min_ge = lambda cand: sweep(cand, lambda a, s, cb: jnp.minimum(a, jnp.where(s >= cb, s, inf)),
                                jnp.full((tq, LANES), inf, F32), jnp.min)

    def active(clo, tie):
        return jnp.logical_and(clo > kf, tie == 0.0)

    def any_active(clo, tie):
        return (jnp.max(jnp.where(active(clo, tie), 1.0, 0.0)) > 0.0).astype(I32)

    def cheap_body(st):
        n, _, lo, hi, clo, tie = st
        act = active(clo, tie)
        mid = 0.5 * lo + 0.5 * hi
        cand = jnp.where(mid > lo, mid, hi)
        c = count_ge(cand)
        ge = c >= kf
        up = jnp.logical_and(act, ge)
        dn = jnp.logical_and(act, jnp.logical_not(ge))
        lo = jnp.where(up, cand, lo)
        clo = jnp.where(up, c, clo)
        hi = jnp.where(dn, cand, hi)
        return n + 1, any_active(clo, tie), lo, hi, clo, tie

    def snap(st):
        lo, clo, tie, need = st
        act = active(clo, tie)
        p = min_ge(lo)
        cgt = count_gt(p)
        res = jnp.logical_and(act, cgt < kf)
        return (jnp.where(act, p, lo), clo, jnp.where(res, 1.0, tie), jnp.where(res, kf - cgt, need))

    def outer_body(st):
        _, lo, hi, clo, tie, need = st
        _, go, lo, hi, clo, tie = lax.while_loop(
            lambda s: jnp.logical_and(s[0] < CHEAP_ITERS, s[1] > 0), cheap_body,
            (jnp.int32(0), jnp.int32(1), lo, hi, clo, tie))
        lo, clo, tie, need = lax.cond(go > 0, snap, lambda s: s, (lo, clo, tie, need))
        return any_active(clo, tie), lo, hi, clo, tie, need

    lo0 = jnp.min(mn, axis=1, keepdims=True)
    hi0 = jnp.max(mx, axis=1, keepdims=True)
    clo0 = (i * tq + lax.broadcasted_iota(I32, (tq, 1), 0) + 1).astype(F32)
    zcol = jnp.zeros((tq, 1), F32)
    _, thr, _, _, tie, need = lax.while_loop(
        lambda s: s[0] > 0, outer_body, (any_active(clo0, zcol), lo0, hi0, clo0, zcol, zcol))

    @pl.when(jnp.max(tie) > 0.0)
    def _():
        rr = lax.broadcasted_iota(I32, (LANES, LANES), 0)
        cc = lax.broadcasted_iota(I32, (LANES, LANES), 1)
        upper = jnp.where(rr <= cc, 1.0, 0.0).astype(BF16)
        thr_b = jnp.broadcast_to(thr, (tq, LANES))
        need_b = jnp.broadcast_to(need, (tq, LANES))
        tie_b = jnp.broadcast_to(tie, (tq, LANES))

        def tie_body(j, seen):
            for c in range(nl):
                s = sc_ref[j, :, lanes(c)]
                eqf = jnp.where(s == thr_b, 1.0, 0.0) * tie_b
                pref = jnp.dot(eqf.astype(BF16), upper, preferred_element_type=F32) + seen
                rej = eqf * jnp.where(pref > need_b, 1.0, 0.0)
                sc_ref[j, :, lanes(c)] = jnp.where(rej > 0.0, -inf, s)
                seen = seen + jnp.sum(eqf, axis=1, keepdims=True)
            return seen

        lax.fori_loop(0, nk, tie_body, zcol)

    thr_b = jnp.broadcast_to(thr, (tq, LANES))
    q2 = q_ref[...].reshape(rows, HEAD_DIM)
    m_ref[...] = jnp.full(m_ref.shape, -inf, F32)
    acc_ref[...] = jnp.zeros(acc_ref.shape, F32)

    def attn_body(j, carry):
        base = pl.multiple_of(j * sk, sk)
        for c in range(nl):
            bias_ref[:, lanes(c)] = jnp.where(sc_ref[j, :, lanes(c)] >= thr_b, 0.0, -inf)
        kc = k_ref[pl.ds(base, sk), :]
        s_ref[...] = lax.dot_general(q2, kc, nt_dims, preferred_element_type=F32)
        for h in range(N_HEADS):
            mx = None
            for c in range(nl):
                t = s_ref[head_rows(h), lanes(c)] + bias_ref[:, lanes(c)]
                mx = t if mx is None else jnp.maximum(mx, t)
            m_old = m_ref[h]
            m_new = jnp.maximum(m_old, jnp.max(mx, axis=1, keepdims=True))
            m_safe = jnp.where(m_new == -inf, 0.0, m_new)
            for c in range(nl):
                t = s_ref[head_rows(h), lanes(c)] + bias_ref[:, lanes(c)]
                p_ref[head_rows(h), lanes(c)] = jnp.exp(t - m_safe).astype(BF16)
            alpha_ref[head_rows(h), :] = jnp.exp(m_old - m_safe)
            m_ref[h] = m_new
        vc = v_ref[pl.ds(base, sk), :]
        pv = jnp.dot(p_ref[...], vc, preferred_element_type=F32)
        acc_ref[...] = acc_ref[...] * alpha_ref[...] + pv
        return carry

    lax.fori_loop(0, nk, attn_body, 0)

    acc = acc_ref[...]
    out = acc[:, :HEAD_DIM] / acc[:, HEAD_DIM:HEAD_DIM + 1]
    for h in range(N_HEADS):
        o_ref[:, h * HEAD_DIM:(h + 1) * HEAD_DIM] = out[head_rows(h)].astype(BF16)


def _sparse_attention(q, k, v, iq, ik, iw, tq, sk):
    b, _, l, _ = q.shape
    ksel = min(TOPK_MAX, l // 4)
    rows = N_HEADS * tq
    whole = lambda n: pl.BlockSpec((None, l, n), lambda bi, i: (bi, 0, 0))
    head = lambda n: pl.BlockSpec((None, N_HEADS, tq, n), lambda bi, i: (bi, 0, i, 0))
    return pl.pallas_call(
        functools.partial(_dsa_kernel, tq=tq, sk=sk, ksel=ksel),
        grid=(b, l // tq),
        in_specs=[head(HEAD_DIM), whole(HEAD_DIM), whole(LANES), head(IDX_DIM), whole(IDX_DIM),
                  pl.BlockSpec((None, tq, IDX_HEADS), lambda bi, i: (bi, i, 0))],
        out_specs=pl.BlockSpec((None, tq, ATTN_W), lambda bi, i: (bi, i, 0)),
        out_shape=jax.ShapeDtypeStruct((b, l, ATTN_W), BF16),
        scratch_shapes=[
            pltpu.VMEM((l // sk, tq, sk), F32),
            pltpu.VMEM((IDX_HEADS, tq, LANES), F32),
            pltpu.VMEM((tq, sk), F32),
            pltpu.VMEM((rows, sk), F32),
            pltpu.VMEM((rows, sk), BF16),
            pltpu.VMEM((N_HEADS, tq, LANES), F32),
            pltpu.VMEM((rows, LANES), F32),
            pltpu.VMEM((rows, LANES), F32),
        ],
        compiler_params=_cparams(("arbitrary", "arbitrary")),
        name="sparse_attention",
    )(q, k, v, iq, ik, iw)


def _merge_kernel(x_ref, ca_ref, at_ref, sc1_ref, sh1_ref, gt1_ref, sc2_ref, sh2_ref, g1_ref, g2_ref,
                  wg_ref, wco_ref, bco_ref, wao_ref, wout_ref, wr_ref, br_ref,
                  xo_ref, h2_ref, cmb_ref):
    x = x_ref[...]
    d = x.shape[-1]
    hb = _rms_mod(x, g1_ref[...], sc1_ref[...], sh1_ref[...]).astype(BF16)
    gates = jax.nn.sigmoid(jnp.dot(hb, wg_ref[...], preferred_element_type=F32))
    y_conv = jnp.dot(ca_ref[...], wco_ref[...], preferred_element_type=F32) + bco_ref[...]
    y_attn = jnp.dot(at_ref[...], wao_ref[...], preferred_element_type=F32)
    merged = gates[:, :d] * y_conv + gates[:, d:] * y_attn
    xo = x + gt1_ref[...] * jnp.dot(merged.astype(BF16), wout_ref[...], preferred_element_type=F32)
    xo_ref[...] = xo

    h2 = _rms_mod(xo, g2_ref[...], sc2_ref[...], sh2_ref[...]).astype(BF16)
    h2_ref[...] = h2

    lg = jnp.dot(h2, wr_ref[...], preferred_element_type=F32) + br_ref[...]
    lane = lax.broadcasted_iota(I32, lg.shape, 1)
    big = jnp.int32(LANES)
    is_g = lane < N_GROUPS
    gl = jnp.where(is_g, lg, -jnp.inf)
    gmax = jnp.max(gl, axis=-1, keepdims=True)
    g_sel = jnp.min(jnp.where(gl == gmax, lane, big), axis=-1, keepdims=True)
    p_group = 1.0 / jnp.sum(jnp.where(is_g, jnp.exp(gl - gmax), 0.0), axis=-1, keepdims=True)
    eid = lane - N_GROUPS
    in_grp = jnp.logical_and(jnp.logical_and(eid >= 0, eid < N_EXPERTS),
                             (eid // EXPERTS_PER_GROUP) == g_sel)
    el = jnp.where(in_grp, lg, -jnp.inf)
    v1 = jnp.max(el, axis=-1, keepdims=True)
    i1 = jnp.min(jnp.where(el == v1, lane, big), axis=-1, keepdims=True)
    el2 = jnp.where(lane == i1, -jnp.inf, el)
    v2 = jnp.max(el2, axis=-1, keepdims=True)
    i2 = jnp.min(jnp.where(el2 == v2, lane, big), axis=-1, keepdims=True)
    e21 = jnp.exp(v2 - v1)
    p1 = 1.0 / (1.0 + e21)
    p2 = e21 / (1.0 + e21)
    cmb_ref[...] = p_group * (jnp.where(lane == i1, p1, 0.0) + jnp.where(lane == i2, p2, 0.0))


def _merge(x, conv_act, attn, mods, g1, g2, wts, tm):
    b, l, d = x.shape
    full = lambda a: pl.BlockSpec(a.shape, lambda bi, i: (0,) * a.ndim)
    tok = lambda n: pl.BlockSpec((None, tm, n), lambda bi, i: (bi, i, 0))
    vec = pl.BlockSpec((None, 1, d), lambda bi, i: (bi, 0, 0))
    return pl.pallas_call(
        _merge_kernel,
        grid=(b, l // tm),
        in_specs=[tok(d), tok(CONV_CH), tok(ATTN_W)] + [vec] * 5 + [full(g1), full(g2)]
                 + [full(w) for w in wts],
        out_specs=[tok(d), tok(d), tok(LANES)],
        out_shape=[jax.ShapeDtypeStruct((b, l, d), F32),
                   jax.ShapeDtypeStruct((b, l, d), BF16),
                   jax.ShapeDtypeStruct((b, l, LANES), F32)],
        compiler_params=_cparams(("arbitrary", "arbitrary")),
        name="merge_route",
    )(x, conv_act, attn, *mods, g1, g2, *wts)


def _moe_kernel(h2_ref, cmb_ref, x_ref, gt2_ref, wg_ref, wu_ref, wd_ref, o_ref, acc_ref):
    e = pl.program_id(2)

    @pl.when(e == 0)
    def _():
        acc_ref[...] = jnp.zeros(acc_ref.shape, F32)

    h = h2_ref[...]
    a = jnp.dot(h, wg_ref[...], preferred_element_type=F32)
    u = jnp.dot(h, wu_ref[...], preferred_element_type=F32)
    hid = (a * jax.nn.sigmoid(a)) * u
    y = jnp.dot(hid.astype(BF16), wd_ref[...], preferred_element_type=F32)
    cmb = cmb_ref[...]
    lane = lax.broadcasted_iota(I32, cmb.shape, 1)
    ce = jnp.sum(jnp.where(lane == e + N_GROUPS, cmb, 0.0), axis=-1, keepdims=True)
    acc_ref[...] += ce * y

    @pl.when(e == pl.num_programs(2) - 1)
    def _():
        o_ref[...] = x_ref[...] + gt2_ref[...] * acc_ref[...]


def _moe(h2, cmb, x, gate2, w_gate, w_up, w_down, tm):
    b, l, d = x.shape
    ne, _, de = w_gate.shape
    tok = lambda n: pl.BlockSpec((None, tm, n), lambda bi, i, e: (bi, i, 0))
    return pl.pallas_call(
        _moe_kernel,
        grid=(b, l // tm, ne),
        in_specs=[tok(d), tok(LANES), tok(d),
                  pl.BlockSpec((None, 1, d), lambda bi, i, e: (bi, 0, 0)),
                  pl.BlockSpec((None, d, de), lambda bi, i, e: (e, 0, 0)),
                  pl.BlockSpec((None, d, de), lambda bi, i, e: (e, 0, 0)),
                  pl.BlockSpec((None, de, d), lambda bi, i, e: (e, 0, 0))],
        out_specs=tok(d),
        out_shape=jax.ShapeDtypeStruct((b, l, d), F32),
        scratch_shapes=[pltpu.VMEM((tm, d), F32)],
        compiler_params=_cparams(("arbitrary", "arbitrary", "arbitrary")),
        name="moe_experts",
    )(h2, cmb, x, gate2, w_gate, w_up, w_down)


def _rope_tables(length):
    pos = jnp.arange(length, dtype=F32)

    def cs(dim):
        inv = ROPE_THETA ** (-jnp.arange(0, dim, 2, dtype=F32) / dim)
        ang = pos[:, None] * inv[None, :]
        c, s = jnp.cos(ang), jnp.sin(ang)
        return jnp.concatenate([c, c], -1), jnp.concatenate([-s, s], -1)

    ch, sh = cs(HEAD_DIM)
    ci, si = cs(IDX_DIM)
    pad1 = lambda a: jnp.concatenate([a, jnp.ones((length, LANES - a.shape[1]), F32)], -1)
    pad0 = lambda a: jnp.concatenate([a, jnp.zeros((length, LANES - a.shape[1]), F32)], -1)
    return (jnp.tile(ch, (1, N_HEADS)), jnp.tile(sh, (1, N_HEADS)), pad1(ch), pad0(sh),
            jnp.tile(ci, (1, IDX_HEADS)), jnp.tile(si, (1, IDX_HEADS)), pad1(ci), pad0(si))


def _forward(x, c, w_ada, b_ada, g_norm1, w_in, q_norm_g, k_norm_g, conv_w, conv_b,
             conv_ln_g, conv_ln_b, w_conv_out, b_conv_out, w_attn_out, w_out, g_norm2,
             w_group, b_group, w_router, b_router, w_gate, w_up, w_down,
             *, tm, tl, tq, sk, tmoe):
    b, l, d = x.shape
    depth = w_ada.shape[0]
    tabs = _rope_tables(l)
    ones_bd = jnp.asarray(np.kron(np.eye(N_HEADS), np.ones((HEAD_DIM, HEAD_DIM))), BF16)
    mod = _modulation(c, w_ada, b_ada)
    o_q, o_k, o_iq, o_ik = 2 * CONV_CH, 2 * CONV_CH + ATTN_W, 0, 0
    o_k = o_q + ATTN_W
    o_iq = o_k + 2 * HEAD_DIM
    o_ik = o_iq + IDX_HEADS * IDX_DIM
    o_g = o_ik + IDX_DIM + IDX_HEADS
    for li in range(depth):
        shift1, scale1, gate1, shift2, scale2, gate2 = [
            m.reshape(b, 1, d) for m in jnp.split(mod[li], N_MOD, axis=-1)]
        w = w_in[li]
        wik = jnp.zeros((d, LANES), F32).at[:, :o_g - o_ik].set(w[:, o_ik:o_g])
        in_wts = (w[:, :o_q].astype(BF16), w[:, o_q:o_k].astype(BF16), w[:, o_k:o_iq].astype(BF16),
                  w[:, o_iq:o_ik].astype(BF16), wik.astype(BF16), ones_bd,
                  jnp.tile(q_norm_g[li], N_HEADS).reshape(1, ATTN_W),
                  jnp.concatenate([k_norm_g[li], jnp.ones((LANES - HEAD_DIM,), F32)]).reshape(1, LANES))
        g1 = g_norm1[li].reshape(1, d)
        g2 = g_norm2[li].reshape(1, d)
        uglu, q, k, v, iq, ik, iw = _in_projection(x, scale1, shift1, g1, in_wts, tabs, tm)
        conv_act = _conv_branch(uglu, conv_w[li], conv_b[li], conv_ln_g[li], conv_ln_b[li], tl)
        attn = _sparse_attention(q, k, v, iq, ik, iw, tq, sk)
        wr = jnp.zeros((d, LANES), F32).at[:, :N_GROUPS].set(w_group[li])
        wr = wr.at[:, N_GROUPS:N_GROUPS + N_EXPERTS].set(w_router[li])
        br = jnp.zeros((1, LANES), F32).at[0, :N_GROUPS].set(b_group[li])
        br = br.at[0, N_GROUPS:N_GROUPS + N_EXPERTS].set(b_router[li])
        merge_wts = (w[:, o_g:].astype(BF16), w_conv_out[li].astype(BF16), b_conv_out[li].reshape(1, d),
                     w_attn_out[li].astype(BF16), w_out[li].astype(BF16), wr.astype(BF16), br)
        x, h2, cmb = _merge(x, conv_act, attn, (scale1, shift1, gate1, scale2, shift2), g1, g2,
                            merge_wts, tm)
        x = _moe(h2, cmb, x, gate2, w_gate[li].astype(BF16), w_up[li].astype(BF16),
                 w_down[li].astype(BF16), tmoe)
    return x


def kernel(x, c, w_ada, b_ada, g_norm1, w_in, q_norm_g, k_norm_g, conv_w, conv_b, conv_ln_g, conv_ln_b, w_conv_out, b_conv_out, w_attn_out, w_out, g_norm2, w_group, b_group, w_router, b_router, w_gate, w_up, w_down):
    return _forward(x, c, w_ada, b_ada, g_norm1, w_in, q_norm_g, k_norm_g, conv_w, conv_b,
                    conv_ln_g, conv_ln_b, w_conv_out, b_conv_out, w_attn_out, w_out, g_norm2,
                    w_group, b_group, w_router, b_router, w_gate, w_up, w_down,
                    tm=512, tl=512, tq=128, sk=512, tmoe=1024)
```

```python
import functools

import numpy as np
import jax
import jax.numpy as jnp
from jax import lax
from jax.experimental import pallas as pl
from jax.experimental.pallas import tpu as pltpu

F32 = jnp.float32
BF16 = jnp.bfloat16
I32 = jnp.int32

N_MOD = 6
CONV_CH = 512
CONV_WIDTH = 31
N_HEADS = 8
HEAD_DIM = 64
ATTN_W = N_HEADS * HEAD_DIM
IDX_HEADS = 8
IDX_DIM = 32
TOPK_MAX = 256
ROPE_THETA = 10000.0
INDEX_SCALE = (IDX_DIM ** -0.5) * (IDX_HEADS ** -0.5)
N_GROUPS = 4
EXPERTS_PER_GROUP = 8
N_EXPERTS = N_GROUPS * EXPERTS_PER_GROUP
EPS = 1e-6

LANES = 128
VMEM_LIMIT = 48 * 1024 * 1024
IDX_SUB = 256
SWEEP_ROWS = 128
CHEAP_ITERS = 20
Q_SCALE = (HEAD_DIM ** -0.5) * 1.4426950408889634
HALO = 32


def _cparams(sem):
    return pltpu.CompilerParams(dimension_semantics=sem, vmem_limit_bytes=VMEM_LIMIT)


def _rms_mod(x, g, scale, shift):
    ms = jnp.mean(x * x, axis=-1, keepdims=True)
    y = x * lax.rsqrt(ms + EPS) * g
    return y * (1.0 + scale) + shift


def _rot_half(x, half):
    n = x.shape[-1]
    lane = lax.broadcasted_iota(I32, x.shape, x.ndim - 1)
    first = (lane % (2 * half)) < half
    return jnp.where(first, pltpu.roll(x, n - half, x.ndim - 1), pltpu.roll(x, half, x.ndim - 1))


def _mod_kernel(c_ref, w_ref, b_ref, o_ref):
    c = c_ref[...]
    sc = (c * jax.nn.sigmoid(c)).astype(BF16)
    o_ref[...] = jnp.dot(sc, w_ref[...].astype(BF16), preferred_element_type=F32) + b_ref[...]


def _modulation(c, w_ada, b_ada):
    depth, d, n = w_ada.shape
    b = c.shape[0]
    rows = 8
    c_pad = jnp.zeros((rows, d), F32).at[:b].set(c)
    tn = 1536
    out = pl.pallas_call(
        _mod_kernel,
        grid=(depth, n // tn),
        in_specs=[
            pl.BlockSpec((rows, d), lambda l, j: (0, 0)),
            pl.BlockSpec((None, d, tn), lambda l, j: (l, 0, j)),
            pl.BlockSpec((None, 1, tn), lambda l, j: (l, 0, j)),
        ],
        out_specs=pl.BlockSpec((None, rows, tn), lambda l, j: (l, 0, j)),
        out_shape=jax.ShapeDtypeStruct((depth, rows, n), F32),
        compiler_params=_cparams(("arbitrary", "arbitrary")),
        name="adaln_mod",
    )(c_pad, w_ada, b_ada.reshape(depth, 1, n))
    return out[:, :b]


def _inproj_kernel(x_ref, sc_ref, sh_ref, g1_ref, wc_ref, wq_ref, wkv_ref, wiq_ref, wik_ref, ones_ref,
                   qg_ref, kg_ref, cq_ref, sq_ref, ckv_ref, skv_ref, ci_ref, si_ref, cik_ref, sik_ref,
                   uglu_ref, q_ref, k_ref, v_ref, iq_ref, ik_ref, iw_ref):
    hb = _rms_mod(x_ref[...], g1_ref[...], sc_ref[...], sh_ref[...]).astype(BF16)

    uc = jnp.dot(hb, wc_ref[...], preferred_element_type=F32)
    uglu_ref[...] = uc[:, :CONV_CH] * jax.nn.sigmoid(uc[:, CONV_CH:])

    uq = jnp.dot(hb, wq_ref[...], preferred_element_type=F32)
    sq = uq * uq
    sq_hi = sq.astype(BF16)
    sq_lo = (sq - sq_hi.astype(F32)).astype(BF16)
    ssq = (jnp.dot(sq_hi, ones_ref[...], preferred_element_type=F32)
           + jnp.dot(sq_lo, ones_ref[...], preferred_element_type=F32))
    qn = uq * lax.rsqrt(ssq * (1.0 / HEAD_DIM) + EPS) * qg_ref[...]
    qr = (qn * cq_ref[...] + _rot_half(qn, HEAD_DIM // 2) * sq_ref[...]) * Q_SCALE
    for h in range(N_HEADS):
        q_ref[h] = qr[:, h * HEAD_DIM:(h + 1) * HEAD_DIM].astype(BF16)

    ukv = jnp.dot(hb, wkv_ref[...], preferred_element_type=F32)
    lane = lax.broadcasted_iota(I32, ukv.shape, 1)
    is_k = lane < HEAD_DIM
    ssk = jnp.sum(jnp.where(is_k, ukv * ukv, 0.0), axis=-1, keepdims=True)
    kn = ukv * lax.rsqrt(ssk * (1.0 / HEAD_DIM) + EPS) * kg_ref[...]
    kr = kn * ckv_ref[...] + _rot_half(kn, HEAD_DIM // 2) * skv_ref[...]
    k_ref[...] = kr[:, :HEAD_DIM].astype(BF16)
    v_ext = jnp.where(is_k, pltpu.roll(ukv, HEAD_DIM, 1), jnp.where(lane == HEAD_DIM, 1.0, 0.0))
    v_ref[...] = v_ext.astype(BF16)

    uiq = jnp.dot(hb, wiq_ref[...], preferred_element_type=F32)
    iqr = uiq * ci_ref[...] + _rot_half(uiq, IDX_DIM // 2) * si_ref[...]
    for h in range(IDX_HEADS):
        iq_ref[h] = iqr[:, h * IDX_DIM:(h + 1) * IDX_DIM].astype(BF16)

    uik = jnp.dot(hb, wik_ref[...], preferred_element_type=F32)
    ikr = uik * cik_ref[...] + _rot_half(uik, IDX_DIM // 2) * sik_ref[...]
    ik_ref[...] = ikr[:, :IDX_DIM].astype(BF16)
    iw_ref[...] = ikr[:, IDX_DIM:IDX_DIM + IDX_HEADS] * INDEX_SCALE


def _in_projection(x, scale1, shift1, g1, wts, tabs, tm):
    b, l, d = x.shape
    wc, wq, wkv, wiq, wik, ones_bd, qg, kg = wts
    grid = (b, l // tm)
    full = lambda a: pl.BlockSpec(a.shape, lambda bi, i: (0,) * a.ndim)
    tok = lambda n: pl.BlockSpec((None, tm, n), lambda bi, i: (bi, i, 0))
    vec = pl.BlockSpec((None, 1, d), lambda bi, i: (bi, 0, 0))
    tab = lambda a: pl.BlockSpec((tm, a.shape[1]), lambda bi, i: (i, 0))
    head = lambda n: pl.BlockSpec((None, N_HEADS, tm, n), lambda bi, i: (bi, 0, i, 0))
    return pl.pallas_call(
        _inproj_kernel,
        grid=grid,
        in_specs=[tok(d), vec, vec, full(g1), full(wc), full(wq), full(wkv), full(wiq), full(wik),
                  full(ones_bd), full(qg), full(kg)] + [tab(t) for t in tabs],
        out_specs=[tok(CONV_CH), head(HEAD_DIM), tok(HEAD_DIM), tok(LANES), head(IDX_DIM),
                   tok(IDX_DIM), tok(IDX_HEADS)],
        out_shape=[
            jax.ShapeDtypeStruct((b, l, CONV_CH), F32),
            jax.ShapeDtypeStruct((b, N_HEADS, l, HEAD_DIM), BF16),
            jax.ShapeDtypeStruct((b, l, HEAD_DIM), BF16),
            jax.ShapeDtypeStruct((b, l, LANES), BF16),
            jax.ShapeDtypeStruct((b, IDX_HEADS, l, IDX_DIM), BF16),
            jax.ShapeDtypeStruct((b, l, IDX_DIM), BF16),
            jax.ShapeDtypeStruct((b, l, IDX_HEADS), F32),
        ],
        compiler_params=_cparams(("arbitrary", "arbitrary")),
        name="in_projection",
    )(x, scale1, shift1, g1, wc, wq, wkv, wiq, wik, ones_bd, qg, kg, *tabs)


def _conv_kernel(u_ref, halo_ref, w_ref, cb_ref, g_ref, b_ref, o_ref, buf_ref, *, tl, rows):
    i = pl.program_id(1)
    buf_ref[0:HALO] = jnp.where(i > 0, halo_ref[...], 0.0)
    buf_ref[HALO:] = u_ref[...]
    first = HALO - (CONV_WIDTH - 1)
    for r in range(tl // rows):
        acc = jnp.zeros((rows, CONV_CH), F32) + cb_ref[...]
        for j in range(CONV_WIDTH):
            s = r * rows + first + j
            acc = acc + buf_ref[s:s + rows, :] * w_ref[j:j + 1, :]
        mu = jnp.mean(acc, axis=-1, keepdims=True)
        yc = acc - mu
        var = jnp.mean(yc * yc, axis=-1, keepdims=True)
        yn = yc * lax.rsqrt(var + EPS) * g_ref[...] + b_ref[...]
        o_ref[r * rows:(r + 1) * rows, :] = (yn * jax.nn.sigmoid(yn)).astype(BF16)


def _conv_branch(uglu, conv_w, conv_b, ln_g, ln_b, tl):
    b, l, ch = uglu.shape
    per = tl // HALO
    w_pad = jnp.zeros((HALO, ch), F32).at[:CONV_WIDTH].set(conv_w)
    row = lambda a: a.reshape(1, ch)
    full = lambda r: pl.BlockSpec((r, ch), lambda bi, i: (0, 0))
    return pl.pallas_call(
        functools.partial(_conv_kernel, tl=tl, rows=64),
        grid=(b, l // tl),
        in_specs=[
            pl.BlockSpec((None, tl, ch), lambda bi, i: (bi, i, 0)),
            pl.BlockSpec((None, HALO, ch), lambda bi, i: (bi, jnp.maximum(i * per - 1, 0), 0)),
            full(HALO), full(1), full(1), full(1),
        ],
        out_specs=pl.BlockSpec((None, tl, ch), lambda bi, i: (bi, i, 0)),
        out_shape=jax.ShapeDtypeStruct((b, l, ch), BF16),
        scratch_shapes=[pltpu.VMEM((tl + HALO, ch), F32)],
        compiler_params=_cparams(("arbitrary", "arbitrary")),
        name="conformer_conv",
    )(uglu, uglu, w_pad, row(conv_b), row(ln_g), row(ln_b))


def _dsa_kernel(q_ref, k_ref, v_ref, iq_ref, ik_ref, iw_ref, o_ref,
                sc_ref, wb_ref, cb_ref, mn_ref, mx_ref, bias_ref, s_ref, p_ref, m_ref, alpha_ref, acc_ref,
                *, tq, sk, ksel):
    i = pl.program_id(1)
    nk = ((i + 1) * tq + sk - 1) // sk
    nl = sk // LANES
    hk = sk // 2
    nlh = hk // LANES
    rows = N_HEADS * tq
    nt_dims = (((1,), (1,)), ((), ()))
    kf = float(ksel)
    inf = jnp.inf

    def lanes(c):
        return slice(c * LANES, (c + 1) * LANES)

    def head_rows(h):
        return slice(h * tq, (h + 1) * tq)

    iw = iw_ref[...]
    for h in range(IDX_HEADS):
        wb_ref[h] = jnp.broadcast_to(iw[:, h:h + 1], (tq, LANES))
    iq2 = iq_ref[...].reshape(IDX_HEADS * tq, IDX_DIM)
    qpos = i * tq + lax.broadcasted_iota(I32, (tq, LANES), 0)
    lane = lax.broadcasted_iota(I32, (tq, LANES), 1)

    mn_ref[...] = jnp.full((tq, LANES), inf, F32)
    mx_ref[...] = jnp.full((tq, LANES), -inf, F32)

    def score_body(j, carry):
        base = pl.multiple_of(j * sk, sk)
        for c2 in range(sk // IDX_SUB):
            ikc = ik_ref[pl.ds(base + c2 * IDX_SUB, IDX_SUB), :]
            r = lax.dot_general(iq2, ikc, nt_dims, preferred_element_type=F32)
            for half in range(IDX_SUB // LANES):
                c = c2 * (IDX_SUB // LANES) + half
                sc = None
                for h in range(IDX_HEADS):
                    t = jnp.maximum(r[head_rows(h), lanes(half)], 0.0) * wb_ref[h]
                    sc = t if sc is None else sc + t
                mn_ref[...] = jnp.minimum(mn_ref[...], sc)
                mx_ref[...] = jnp.maximum(mx_ref[...], sc)
                kpos = base + c * LANES + lane
                sc_ref[j, :, lanes(c)] = jnp.where(kpos <= qpos, sc, -inf)
        return carry

    lax.fori_loop(0, nk, score_body, 0)

    def sweep(cand, fn, init, reduce):
        outs = []
        for g in range(tq // SWEEP_ROWS):
            rs = slice(g * SWEEP_ROWS, (g + 1) * SWEEP_ROWS)
            cb = jnp.broadcast_to(cand[rs], (SWEEP_ROWS, LANES))

            def body(j, acc, rs=rs, cb=cb):
                for c in range(nl):
                    acc = fn(acc, sc_ref[j, rs, lanes(c)], cb)
                return acc

            acc = lax.fori_loop(0, nk, body, jnp.full((SWEEP_ROWS, LANES), init, F32))
            outs.append(reduce(acc, axis=1, keepdims=True))
        return jnp.concatenate(outs, axis=0)

    count_ge = lambda cand: sweep(cand, lambda a, s, cb: a + jnp.where(s >= cb, 1.0, 0.0), 0.0, jnp.sum)
    count_gt = lambda cand: sweep(cand, lambda a, s, cb: a + jnp.where(s > cb, 1.0, 0.0), 0.0, jnp.sum)
    min_ge = lambda cand: sweep(cand, lambda a, s, cb: jnp.minimum(a, jnp.where(s >= cb, s, inf)),
                                inf, jnp.min)

    def active(clo, tie):
        return jnp.logical_and(clo > kf, tie == 0.0)

    def any_active(clo, tie):
        return (jnp.max(jnp.where(active(clo, tie), 1.0, 0.0)) > 0.0).astype(I32)

    def cheap_body(st):
        n, _, lo, hi, clo, tie = st
        go = any_active(clo, tie)
        mid = 0.5 * lo + 0.5 * hi
        cand = jnp.where(mid > lo, mid, hi)
        c = count_ge(cand)
        ge = c >= kf
        return (n + 1, go, jnp.where(ge, cand, lo), jnp.where(ge, hi, cand), jnp.where(ge, c, clo), tie)

    def snap(st):
        lo, clo, tie, need = st
        act = active(clo, tie)
        p = min_ge(lo)
        cgt = count_gt(p)
        res = jnp.logical_and(act, cgt < kf)
        return (jnp.where(act, p, lo), clo, jnp.where(res, 1.0, tie), jnp.where(res, kf - cgt, need))

    def outer_body(st):
        _, lo, hi, clo, tie, need = st
        _, _, lo, hi, clo, tie = lax.while_loop(
            lambda s: jnp.logical_and(s[0] < CHEAP_ITERS, s[1] > 0), cheap_body,
            (jnp.int32(0), jnp.int32(1), lo, hi, clo, tie))
        lo, clo, tie, need = lax.cond(any_active(clo, tie) > 0, snap, lambda s: s, (lo, clo, tie, need))
        return any_active(clo, tie), lo, hi, clo, tie, need

    lo0 = jnp.min(mn_ref[...], axis=1, keepdims=True)
    hi0 = jnp.max(mx_ref[...], axis=1, keepdims=True)
    clo0 = (i * tq + lax.broadcasted_iota(I32, (tq, 1), 0) + 1).astype(F32)
    zcol = jnp.zeros((tq, 1), F32)
    _, thr, _, _, tie, need = lax.while_loop(
        lambda s: s[0] > 0, outer_body, (any_active(clo0, zcol), lo0, hi0, clo0, zcol, zcol))

    @pl.when(jnp.max(tie) > 0.0)
    def _():
        rr = lax.broadcasted_iota(I32, (LANES, LANES), 0)
        cc = lax.broadcasted_iota(I32, (LANES, LANES), 1)
        upper = jnp.where(rr <= cc, 1.0, 0.0).astype(BF16)
        thr_b = jnp.broadcast_to(thr, (tq, LANES))
        need_b = jnp.broadcast_to(need, (tq, LANES))
        tie_b = jnp.broadcast_to(tie, (tq, LANES))

        def tie_body(j, seen):
            for c in range(nl):
                s = sc_ref[j, :, lanes(c)]
                eqf = jnp.where(s == thr_b, 1.0, 0.0) * tie_b
                pref = jnp.dot(eqf.astype(BF16), upper, preferred_element_type=F32) + seen
                rej = eqf * jnp.where(pref > need_b, 1.0, 0.0)
                sc_ref[j, :, lanes(c)] = jnp.where(rej > 0.0, -inf, s)
                seen = seen + jnp.sum(eqf, axis=1, keepdims=True)
            return seen

        lax.fori_loop(0, nk, tie_body, zcol)

    cb_ref[...] = jnp.broadcast_to(thr, (tq, LANES))
    q2 = q_ref[...].reshape(rows, HEAD_DIM)
    m_ref[...] = jnp.full(m_ref.shape, -inf, F32)
    acc_ref[...] = jnp.zeros(acc_ref.shape, F32)
    p_ref[1] = jnp.zeros((rows, hk), BF16)

    def qk(start):
        return lax.dot_general(q2, k_ref[pl.ds(start, hk), :], nt_dims, preferred_element_type=F32)

    def pv(slot, start):
        return jnp.dot(p_ref[slot], v_ref[pl.ds(start, hk), :], preferred_element_type=F32)

    def softmax_half(j, slot):
        for c in range(nlh):
            sel = sc_ref[j, :, lanes(slot * nlh + c)] >= cb_ref[...]
            bias_ref[slot, :, lanes(c)] = jnp.where(sel, 0.0, -inf)
        for h in range(N_HEADS):
            mx = None
            for c in range(nlh):
                t = s_ref[slot, head_rows(h), lanes(c)] + bias_ref[slot, :, lanes(c)]
                mx = t if mx is None else jnp.maximum(mx, t)
            m_old = m_ref[h]
            m_new = jnp.maximum(m_old, jnp.max(mx, axis=1, keepdims=True))
            m_safe = jnp.where(m_new == -inf, 0.0, m_new)
            for c in range(nlh):
                t = s_ref[slot, head_rows(h), lanes(c)] + bias_ref[slot, :, lanes(c)]
                p_ref[slot, head_rows(h), lanes(c)] = jnp.exp2(t - m_safe).astype(BF16)
            alpha_ref[slot, head_rows(h), :] = jnp.exp2(m_old - m_safe)
            m_ref[h] = m_new

    def attn_body(j, carry):
        base = pl.multiple_of(j * sk, sk)
        s_ref[0] = qk(base)
        pv_prev = pv(1, pl.multiple_of(jnp.maximum(base - hk, 0), hk))
        softmax_half(j, 0)
        acc_ref[...] = (acc_ref[...] + pv_prev) * alpha_ref[0]
        s_ref[1] = qk(base + hk)
        pv_prev = pv(0, base)
        softmax_half(j, 1)
        acc_ref[...] = (acc_ref[...] + pv_prev) * alpha_ref[1]
        return carry

    lax.fori_loop(0, nk, attn_body, 0)

    acc = acc_ref[...] + pv(1, pl.multiple_of(nk * sk - hk, hk))
    out = acc[:, :HEAD_DIM] / acc[:, HEAD_DIM:HEAD_DIM + 1]
    for h in range(N_HEADS):
        o_ref[:, h * HEAD_DIM:(h + 1) * HEAD_DIM] = out[head_rows(h)].astype(BF16)


def _sparse_attention(q, k, v, iq, ik, iw, tq, sk):
    b, _, l, _ = q.shape
    ksel = min(TOPK_MAX, l // 4)
    rows = N_HEADS * tq
    hk = sk // 2
    whole = lambda n: pl.BlockSpec((None, l, n), lambda bi, i: (bi, 0, 0))
    head = lambda n: pl.BlockSpec((None, N_HEADS, tq, n), lambda bi, i: (bi, 0, i, 0))
    return pl.pallas_call(
        functools.partial(_dsa_kernel, tq=tq, sk=sk, ksel=ksel),
        grid=(b, l // tq),
        in_specs=[head(HEAD_DIM), whole(HEAD_DIM), whole(LANES), head(IDX_DIM), whole(IDX_DIM),
                  pl.BlockSpec((None, tq, IDX_HEADS), lambda bi, i: (bi, i, 0))],
        out_specs=pl.BlockSpec((None, tq, ATTN_W), lambda bi, i: (bi, i, 0)),
        out_shape=jax.ShapeDtypeStruct((b, l, ATTN_W), BF16),
        scratch_shapes=[
            pltpu.VMEM((l // sk, tq, sk), F32),
            pltpu.VMEM((IDX_HEADS, tq, LANES), F32),
            pltpu.VMEM((tq, LANES), F32),
            pltpu.VMEM((tq, LANES), F32),
            pltpu.VMEM((tq, LANES), F32),
            pltpu.VMEM((2, tq, hk), F32),
            pltpu.VMEM((2, rows, hk), F32),
            pltpu.VMEM((2, rows, hk), BF16),
            pltpu.VMEM((N_HEADS, tq, LANES), F32),
            pltpu.VMEM((2, rows, LANES), F32),
            pltpu.VMEM((rows, LANES), F32),
        ],
        compiler_params=_cparams(("arbitrary", "arbitrary")),
        name="sparse_attention",
    )(q, k, v, iq, ik, iw)


def _merge_kernel(x_ref, ca_ref, at_ref, sc1_ref, sh1_ref, gt1_ref, sc2_ref, sh2_ref, g1_ref, g2_ref,
                  wg_ref, wco_ref, bco_ref, wao_ref, wout_ref, wr_ref, br_ref,
                  xo_ref, h2_ref, cmb_ref):
    x = x_ref[...]
    d = x.shape[-1]
    hb = _rms_mod(x, g1_ref[...], sc1_ref[...], sh1_ref[...]).astype(BF16)
    gates = jax.nn.sigmoid(jnp.dot(hb, wg_ref[...], preferred_element_type=F32))
    y_conv = jnp.dot(ca_ref[...], wco_ref[...], preferred_element_type=F32) + bco_ref[...]
    y_attn = jnp.dot(at_ref[...], wao_ref[...], preferred_element_type=F32)
    merged = gates[:, :d] * y_conv + gates[:, d:] * y_attn
    xo = x + gt1_ref[...] * jnp.dot(merged.astype(BF16), wout_ref[...], preferred_element_type=F32)
    xo_ref[...] = xo

    h2 = _rms_mod(xo, g2_ref[...], sc2_ref[...], sh2_ref[...]).astype(BF16)
    h2_ref[...] = h2

    lg = jnp.dot(h2, wr_ref[...], preferred_element_type=F32) + br_ref[...]
    lane = lax.broadcasted_iota(I32, lg.shape, 1)
    big = jnp.int32(LANES)
    is_g = lane < N_GROUPS
    gl = jnp.where(is_g, lg, -jnp.inf)
    gmax = jnp.max(gl, axis=-1, keepdims=True)
    g_sel = jnp.min(jnp.where(gl == gmax, lane, big), axis=-1, keepdims=True)
    p_group = 1.0 / jnp.sum(jnp.where(is_g, jnp.exp(gl - gmax), 0.0), axis=-1, keepdims=True)
    eid = lane - N_GROUPS
    in_grp = jnp.logical_and(jnp.logical_and(eid >= 0, eid < N_EXPERTS),
                             (eid // EXPERTS_PER_GROUP) == g_sel)
    el = jnp.where(in_grp, lg, -jnp.inf)
    v1 = jnp.max(el, axis=-1, keepdims=True)
    i1 = jnp.min(jnp.where(el == v1, lane, big), axis=-1, keepdims=True)
    el2 = jnp.where(lane == i1, -jnp.inf, el)
    v2 = jnp.max(el2, axis=-1, keepdims=True)
    i2 = jnp.min(jnp.where(el2 == v2, lane, big), axis=-1, keepdims=True)
    e21 = jnp.exp(v2 - v1)
    p1 = 1.0 / (1.0 + e21)
    p2 = e21 / (1.0 + e21)
    cmb_ref[...] = p_group * (jnp.where(lane == i1, p1, 0.0) + jnp.where(lane == i2, p2, 0.0))


def _merge(x, conv_act, attn, mods, g1, g2, wts, tm):
    b, l, d = x.shape
    full = lambda a: pl.BlockSpec(a.shape, lambda bi, i: (0,) * a.ndim)
    tok = lambda n: pl.BlockSpec((None, tm, n), lambda bi, i: (bi, i, 0))
    vec = pl.BlockSpec((None, 1, d), lambda bi, i: (bi, 0, 0))
    return pl.pallas_call(
        _merge_kernel,
        grid=(b, l // tm),
        in_specs=[tok(d), tok(CONV_CH), tok(ATTN_W)] + [vec] * 5 + [full(g1), full(g2)]
                 + [full(w) for w in wts],
        out_specs=[tok(d), tok(d), tok(LANES)],
        out_shape=[jax.ShapeDtypeStruct((b, l, d), F32),
                   jax.ShapeDtypeStruct((b, l, d), BF16),
                   jax.ShapeDtypeStruct((b, l, LANES), F32)],
        compiler_params=_cparams(("arbitrary", "arbitrary")),
        name="merge_route",
    )(x, conv_act, attn, *mods, g1, g2, *wts)


def _moe_kernel(h2_ref, cmb_ref, x_ref, gt2_ref, wg_ref, wu_ref, wd_ref, o_ref, acc_ref):
    e = pl.program_id(2)

    @pl.when(e == 0)
    def _():
        acc_ref[...] = jnp.zeros(acc_ref.shape, F32)

    h = h2_ref[...]
    a = jnp.dot(h, wg_ref[...], preferred_element_type=F32)
    u = jnp.dot(h, wu_ref[...], preferred_element_type=F32)
    hid = (a * jax.nn.sigmoid(a)) * u
    y = jnp.dot(hid.astype(BF16), wd_ref[...], preferred_element_type=F32)
    cmb = cmb_ref[...]
    lane = lax.broadcasted_iota(I32, cmb.shape, 1)
    ce = jnp.sum(jnp.where(lane == e + N_GROUPS, cmb, 0.0), axis=-1, keepdims=True)
    acc_ref[...] += ce * y

    @pl.when(e == pl.num_programs(2) - 1)
    def _():
        o_ref[...] = x_ref[...] + gt2_ref[...] * acc_ref[...]


def _moe(h2, cmb, x, gate2, w_gate, w_up, w_down, tm):
    b, l, d = x.shape
    ne, _, de = w_gate.shape
    tok = lambda n: pl.BlockSpec((None, tm, n), lambda bi, i, e: (bi, i, 0))
    return pl.pallas_call(
        _moe_kernel,
        grid=(b, l // tm, ne),
        in_specs=[tok(d), tok(LANES), tok(d),
                  pl.BlockSpec((None, 1, d), lambda bi, i, e: (bi, 0, 0)),
                  pl.BlockSpec((None, d, de), lambda bi, i, e: (e, 0, 0)),
                  pl.BlockSpec((None, d, de), lambda bi, i, e: (e, 0, 0)),
                  pl.BlockSpec((None, de, d), lambda bi, i, e: (e, 0, 0))],
        out_specs=tok(d),
        out_shape=jax.ShapeDtypeStruct((b, l, d), F32),
        scratch_shapes=[pltpu.VMEM((tm, d), F32)],
        compiler_params=_cparams(("arbitrary", "arbitrary", "arbitrary")),
        name="moe_experts",
    )(h2, cmb, x, gate2, w_gate, w_up, w_down)


def _rope_tables(length):
    pos = jnp.arange(length, dtype=F32)

    def cs(dim):
        inv = ROPE_THETA ** (-jnp.arange(0, dim, 2, dtype=F32) / dim)
        ang = pos[:, None] * inv[None, :]
        c, s = jnp.cos(ang), jnp.sin(ang)
        return jnp.concatenate([c, c], -1), jnp.concatenate([-s, s], -1)

    ch, sh = cs(HEAD_DIM)
    ci, si = cs(IDX_DIM)
    pad1 = lambda a: jnp.concatenate([a, jnp.ones((length, LANES - a.shape[1]), F32)], -1)
    pad0 = lambda a: jnp.concatenate([a, jnp.zeros((length, LANES - a.shape[1]), F32)], -1)
    return (jnp.tile(ch, (1, N_HEADS)), jnp.tile(sh, (1, N_HEADS)), pad1(ch), pad0(sh),
            jnp.tile(ci, (1, IDX_HEADS)), jnp.tile(si, (1, IDX_HEADS)), pad1(ci), pad0(si))


def _forward(x, c, w_ada, b_ada, g_norm1, w_in, q_norm_g, k_norm_g, conv_w, conv_b,
             conv_ln_g, conv_ln_b, w_conv_out, b_conv_out, w_attn_out, w_out, g_norm2,
             w_group, b_group, w_router, b_router, w_gate, w_up, w_down,
             *, tm, tl, tq, sk, tmoe):
    b, l, d = x.shape
    depth = w_ada.shape[0]
    tabs = _rope_tables(l)
    ones_bd = jnp.asarray(np.kron(np.eye(N_HEADS), np.ones((HEAD_DIM, HEAD_DIM))), BF16)
    mod = _modulation(c, w_ada, b_ada)
    o_q, o_k, o_iq, o_ik = 2 * CONV_CH, 2 * CONV_CH + ATTN_W, 0, 0
    o_k = o_q + ATTN_W
    o_iq = o_k + 2 * HEAD_DIM
    o_ik = o_iq + IDX_HEADS * IDX_DIM
    o_g = o_ik + IDX_DIM + IDX_HEADS
    for li in range(depth):
        shift1, scale1, gate1, shift2, scale2, gate2 = [
            m.reshape(b, 1, d) for m in jnp.split(mod[li], N_MOD, axis=-1)]
        w = w_in[li]
        wik = jnp.zeros((d, LANES), F32).at[:, :o_g - o_ik].set(w[:, o_ik:o_g])
        in_wts = (w[:, :o_q].astype(BF16), w[:, o_q:o_k].astype(BF16), w[:, o_k:o_iq].astype(BF16),
                  w[:, o_iq:o_ik].astype(BF16), wik.astype(BF16), ones_bd,
                  jnp.tile(q_norm_g[li], N_HEADS).reshape(1, ATTN_W),
                  jnp.concatenate([k_norm_g[li], jnp.ones((LANES - HEAD_DIM,), F32)]).reshape(1, LANES))
        g1 = g_norm1[li].reshape(1, d)
        g2 = g_norm2[li].reshape(1, d)
        uglu, q, k, v, iq, ik, iw = _in_projection(x, scale1, shift1, g1, in_wts, tabs, tm)
        conv_act = _conv_branch(uglu, conv_w[li], conv_b[li], conv_ln_g[li], conv_ln_b[li], tl)
        attn = _sparse_attention(q, k, v, iq, ik, iw, tq, sk)
        wr = jnp.zeros((d, LANES), F32).at[:, :N_GROUPS].set(w_group[li])
        wr = wr.at[:, N_GROUPS:N_GROUPS + N_EXPERTS].set(w_router[li])
        br = jnp.zeros((1, LANES), F32).at[0, :N_GROUPS].set(b_group[li])
        br = br.at[0, N_GROUPS:N_GROUPS + N_EXPERTS].set(b_router[li])
        merge_wts = (w[:, o_g:].astype(BF16), w_conv_out[li].astype(BF16), b_conv_out[li].reshape(1, d),
                     w_attn_out[li].astype(BF16), w_out[li].astype(BF16), wr.astype(BF16), br)
        x, h2, cmb = _merge(x, conv_act, attn, (scale1, shift1, gate1, scale2, shift2), g1, g2,
                            merge_wts, tm)
        x = _moe(h2, cmb, x, gate2, w_gate[li].astype(BF16), w_up[li].astype(BF16),
                 w_down[li].astype(BF16), tmoe)
    return x


def kernel(x, c, w_ada, b_ada, g_norm1, w_in, q_norm_g, k_norm_g, conv_w, conv_b, conv_ln_g, conv_ln_b, w_conv_out, b_conv_out, w_attn_out, w_out, g_norm2, w_group, b_group, w_router, b_router, w_gate, w_up, w_down):
    return _forward(x, c, w_ada, b_ada, g_norm1, w_in, q_norm_g, k_norm_g, conv_w, conv_b,
                    conv_ln_g, conv_ln_b, w_conv_out, b_conv_out, w_attn_out, w_out, g_norm2,
                    w_group, b_group, w_router, b_router, w_gate, w_up, w_down,
                    tm=512, tl=512, tq=256, sk=512, tmoe=1024)
```

```python
import functools

import numpy as np
import jax
import jax.numpy as jnp
from jax import lax
from jax.experimental import pallas as pl
from jax.experimental.pallas import tpu as pltpu

F32 = jnp.float32
BF16 = jnp.bfloat16
I32 = jnp.int32

N_MOD = 6
CONV_CH = 512
CONV_WIDTH = 31
N_HEADS = 8
HEAD_DIM = 64
ATTN_W = N_HEADS * HEAD_DIM
IDX_HEADS = 8
IDX_DIM = 32
TOPK_MAX = 256
ROPE_THETA = 10000.0
INDEX_SCALE = (IDX_DIM ** -0.5) * (IDX_HEADS ** -0.5)
N_GROUPS = 4
EXPERTS_PER_GROUP = 8
N_EXPERTS = N_GROUPS * EXPERTS_PER_GROUP
EPS = 1e-6

LANES = 128
VMEM_LIMIT = 58 * 1024 * 1024
IDX_SUB = 256
SWEEP_ROWS = 128
CHEAP_ITERS = 20
Q_SCALE = (HEAD_DIM ** -0.5) * 1.4426950408889634
HALO = 32


def _cparams(sem):
    return pltpu.CompilerParams(dimension_semantics=sem, vmem_limit_bytes=VMEM_LIMIT)


def _rms_mod(x, g, scale, shift):
    ms = jnp.mean(x * x, axis=-1, keepdims=True)
    y = x * lax.rsqrt(ms + EPS) * g
    return y * (1.0 + scale) + shift


def _rot_half(x, half):
    n = x.shape[-1]
    lane = lax.broadcasted_iota(I32, x.shape, x.ndim - 1)
    first = (lane % (2 * half)) < half
    return jnp.where(first, pltpu.roll(x, n - half, x.ndim - 1), pltpu.roll(x, half, x.ndim - 1))


def _mod_kernel(c_ref, w_ref, b_ref, o_ref):
    c = c_ref[...]
    sc = (c * jax.nn.sigmoid(c)).astype(BF16)
    o_ref[...] = jnp.dot(sc, w_ref[...].astype(BF16), preferred_element_type=F32) + b_ref[...]


def _modulation(c, w_ada, b_ada):
    depth, d, n = w_ada.shape
    b = c.shape[0]
    rows = 8
    c_pad = jnp.zeros((rows, d), F32).at[:b].set(c)
    tn = 1536
    out = pl.pallas_call(
        _mod_kernel,
        grid=(depth, n // tn),
        in_specs=[
            pl.BlockSpec((rows, d), lambda l, j: (0, 0)),
            pl.BlockSpec((None, d, tn), lambda l, j: (l, 0, j)),
            pl.BlockSpec((None, 1, tn), lambda l, j: (l, 0, j)),
        ],
        out_specs=pl.BlockSpec((None, rows, tn), lambda l, j: (l, 0, j)),
        out_shape=jax.ShapeDtypeStruct((depth, rows, n), F32),
        compiler_params=_cparams(("arbitrary", "arbitrary")),
        name="adaln_mod",
    )(c_pad, w_ada, b_ada.reshape(depth, 1, n))
    return out[:, :b]


def _inproj_kernel(x_ref, sc_ref, sh_ref, g1_ref, wc_ref, wq_ref, wkv_ref, wiq_ref, wik_ref, ones_ref,
                   qg_ref, kg_ref, cq_ref, sq_ref, ckv_ref, skv_ref, ci_ref, si_ref, cik_ref, sik_ref,
                   uglu_ref, q_ref, k_ref, v_ref, iq_ref, ik_ref, iw_ref):
    hb = _rms_mod(x_ref[...], g1_ref[...], sc_ref[...], sh_ref[...]).astype(BF16)

    uc = jnp.dot(hb, wc_ref[...], preferred_element_type=F32)
    uglu_ref[...] = uc[:, :CONV_CH] * jax.nn.sigmoid(uc[:, CONV_CH:])

    uq = jnp.dot(hb, wq_ref[...], preferred_element_type=F32)
    sq = uq * uq
    sq_hi = sq.astype(BF16)
    sq_lo = (sq - sq_hi.astype(F32)).astype(BF16)
    ssq = (jnp.dot(sq_hi, ones_ref[...], preferred_element_type=F32)
           + jnp.dot(sq_lo, ones_ref[...], preferred_element_type=F32))
    qn = uq * lax.rsqrt(ssq * (1.0 / HEAD_DIM) + EPS) * qg_ref[...]
    qr = (qn * cq_ref[...] + _rot_half(qn, HEAD_DIM // 2) * sq_ref[...]) * Q_SCALE
    for h in range(N_HEADS):
        q_ref[h] = qr[:, h * HEAD_DIM:(h + 1) * HEAD_DIM].astype(BF16)

    ukv = jnp.dot(hb, wkv_ref[...], preferred_element_type=F32)
    lane = lax.broadcasted_iota(I32, ukv.shape, 1)
    is_k = lane < HEAD_DIM
    ssk = jnp.sum(jnp.where(is_k, ukv * ukv, 0.0), axis=-1, keepdims=True)
    kn = ukv * lax.rsqrt(ssk * (1.0 / HEAD_DIM) + EPS) * kg_ref[...]
    kr = kn * ckv_ref[...] + _rot_half(kn, HEAD_DIM // 2) * skv_ref[...]
    k_ref[...] = kr[:, :HEAD_DIM].astype(BF16)
    v_ext = jnp.where(is_k, pltpu.roll(ukv, HEAD_DIM, 1), jnp.where(lane == HEAD_DIM, 1.0, 0.0))
    v_ref[...] = v_ext.astype(BF16)

    uiq = jnp.dot(hb, wiq_ref[...], preferred_element_type=F32)
    iqr = uiq * ci_ref[...] + _rot_half(uiq, IDX_DIM // 2) * si_ref[...]
    for h in range(IDX_HEADS):
        iq_ref[h] = iqr[:, h * IDX_DIM:(h + 1) * IDX_DIM].astype(BF16)

    uik = jnp.dot(hb, wik_ref[...], preferred_element_type=F32)
    ikr = uik * cik_ref[...] + _rot_half(uik, IDX_DIM // 2) * sik_ref[...]
    ik_ref[...] = ikr[:, :IDX_DIM].astype(BF16)
    iw_ref[...] = ikr[:, IDX_DIM:IDX_DIM + IDX_HEADS] * INDEX_SCALE


def _in_projection(x, scale1, shift1, g1, wts, tabs, tm):
    b, l, d = x.shape
    wc, wq, wkv, wiq, wik, ones_bd, qg, kg = wts
    grid = (b, l // tm)
    full = lambda a: pl.BlockSpec(a.shape, lambda bi, i: (0,) * a.ndim)
    tok = lambda n: pl.BlockSpec((None, tm, n), lambda bi, i: (bi, i, 0))
    vec = pl.BlockSpec((None, 1, d), lambda bi, i: (bi, 0, 0))
    tab = lambda a: pl.BlockSpec((tm, a.shape[1]), lambda bi, i: (i, 0))
    head = lambda n: pl.BlockSpec((None, N_HEADS, tm, n), lambda bi, i: (bi, 0, i, 0))
    return pl.pallas_call(
        _inproj_kernel,
        grid=grid,
        in_specs=[tok(d), vec, vec, full(g1), full(wc), full(wq), full(wkv), full(wiq), full(wik),
                  full(ones_bd), full(qg), full(kg)] + [tab(t) for t in tabs],
        out_specs=[tok(CONV_CH), head(HEAD_DIM), tok(HEAD_DIM), tok(LANES), head(IDX_DIM),
                   tok(IDX_DIM), tok(IDX_HEADS)],
        out_shape=[
            jax.ShapeDtypeStruct((b, l, CONV_CH), F32),
            jax.ShapeDtypeStruct((b, N_HEADS, l, HEAD_DIM), BF16),
            jax.ShapeDtypeStruct((b, l, HEAD_DIM), BF16),
            jax.ShapeDtypeStruct((b, l, LANES), BF16),
            jax.ShapeDtypeStruct((b, IDX_HEADS, l, IDX_DIM), BF16),
            jax.ShapeDtypeStruct((b, l, IDX_DIM), BF16),
            jax.ShapeDtypeStruct((b, l, IDX_HEADS), F32),
        ],
        compiler_params=_cparams(("arbitrary", "arbitrary")),
        name="in_projection",
    )(x, scale1, shift1, g1, wc, wq, wkv, wiq, wik, ones_bd, qg, kg, *tabs)


def _conv_kernel(u_ref, halo_ref, w_ref, cb_ref, g_ref, b_ref, o_ref, buf_ref, *, tl, rows):
    i = pl.program_id(1)
    buf_ref[0:HALO] = jnp.where(i > 0, halo_ref[...], 0.0)
    buf_ref[HALO:] = u_ref[...]
    first = HALO - (CONV_WIDTH - 1)
    for r in range(tl // rows):
        acc = jnp.zeros((rows, CONV_CH), F32) + cb_ref[...]
        for j in range(CONV_WIDTH):
            s = r * rows + first + j
            acc = acc + buf_ref[s:s + rows, :] * w_ref[j:j + 1, :]
        mu = jnp.mean(acc, axis=-1, keepdims=True)
        yc = acc - mu
        var = jnp.mean(yc * yc, axis=-1, keepdims=True)
        yn = yc * lax.rsqrt(var + EPS) * g_ref[...] + b_ref[...]
        o_ref[r * rows:(r + 1) * rows, :] = (yn * jax.nn.sigmoid(yn)).astype(BF16)


def _conv_branch(uglu, conv_w, conv_b, ln_g, ln_b, tl):
    b, l, ch = uglu.shape
    per = tl // HALO
    w_pad = jnp.zeros((HALO, ch), F32).at[:CONV_WIDTH].set(conv_w)
    row = lambda a: a.reshape(1, ch)
    full = lambda r: pl.BlockSpec((r, ch), lambda bi, i: (0, 0))
    return pl.pallas_call(
        functools.partial(_conv_kernel, tl=tl, rows=64),
        grid=(b, l // tl),
        in_specs=[
            pl.BlockSpec((None, tl, ch), lambda bi, i: (bi, i, 0)),
            pl.BlockSpec((None, HALO, ch), lambda bi, i: (bi, jnp.maximum(i * per - 1, 0), 0)),
            full(HALO), full(1), full(1), full(1),
        ],
        out_specs=pl.BlockSpec((None, tl, ch), lambda bi, i: (bi, i, 0)),
        out_shape=jax.ShapeDtypeStruct((b, l, ch), BF16),
        scratch_shapes=[pltpu.VMEM((tl + HALO, ch), F32)],
        compiler_params=_cparams(("arbitrary", "arbitrary")),
        name="conformer_conv",
    )(uglu, uglu, w_pad, row(conv_b), row(ln_g), row(ln_b))


def _dsa_kernel(q_ref, k_ref, v_ref, iq_ref, ik_ref, iw_ref, o_ref,
                sc_ref, wb_ref, cb_ref, mn_ref, mx_ref, bias_ref, s_ref, p_ref, m_ref, alpha_ref, acc_ref,
                *, tq, sk, ksel):
    i = pl.program_id(1)
    nk = ((i + 1) * tq + sk - 1) // sk
    nl = sk // LANES
    hk = sk // 2
    nlh = hk // LANES
    rows = N_HEADS * tq
    nt_dims = (((1,), (1,)), ((), ()))
    kf = float(ksel)
    inf = jnp.inf

    def lanes(c):
        return slice(c * LANES, (c + 1) * LANES)

    def head_rows(h):
        return slice(h * tq, (h + 1) * tq)

    iw = iw_ref[...]
    for h in range(IDX_HEADS):
        wb_ref[h] = jnp.broadcast_to(iw[:, h:h + 1], (tq, LANES))
    iq2 = iq_ref[...].reshape(IDX_HEADS * tq, IDX_DIM)
    qpos = i * tq + lax.broadcasted_iota(I32, (tq, LANES), 0)
    lane = lax.broadcasted_iota(I32, (tq, LANES), 1)

    mn_ref[...] = jnp.full((tq, LANES), inf, F32)
    mx_ref[...] = jnp.full((tq, LANES), -inf, F32)

    def score_body(j, carry):
        base = pl.multiple_of(j * sk, sk)
        for c2 in range(sk // IDX_SUB):
            ikc = ik_ref[pl.ds(base + c2 * IDX_SUB, IDX_SUB), :]
            r = lax.dot_general(iq2, ikc, nt_dims, preferred_element_type=F32)
            for half in range(IDX_SUB // LANES):
                c = c2 * (IDX_SUB // LANES) + half
                sc = None
                for h in range(IDX_HEADS):
                    t = jnp.maximum(r[head_rows(h), lanes(half)], 0.0) * wb_ref[h]
                    sc = t if sc is None else sc + t
                mn_ref[...] = jnp.minimum(mn_ref[...], sc)
                mx_ref[...] = jnp.maximum(mx_ref[...], sc)
                kpos = base + c * LANES + lane
                sc_ref[j, :, lanes(c)] = jnp.where(kpos <= qpos, sc, -inf)
        return carry

    lax.fori_loop(0, nk, score_body, 0)

    def sweep(cand, fn, init, reduce):
        accs = []
        for g in range(tq // SWEEP_ROWS):
            rs = slice(g * SWEEP_ROWS, (g + 1) * SWEEP_ROWS)
            cb = cand[rs]

            def body(j, acc, rs=rs, cb=cb):
                for c in range(nl):
                    acc = fn(acc, sc_ref[j, rs, lanes(c)], cb)
                return acc

            accs.append(lax.fori_loop(0, nk, body, jnp.full((SWEEP_ROWS, LANES), init, F32)))
        return jnp.concatenate([reduce(a) for a in accs], axis=0)

    ones_sq = jnp.ones((LANES, LANES), BF16)

    def row_count(acc):
        return jnp.dot(acc.astype(BF16), ones_sq, preferred_element_type=F32)

    def row_min(acc):
        return jnp.broadcast_to(jnp.min(acc, axis=1, keepdims=True), acc.shape)

    count_ge = lambda cand: sweep(cand, lambda a, s, cb: a + jnp.where(s >= cb, 1.0, 0.0), 0.0, row_count)
    count_gt = lambda cand: sweep(cand, lambda a, s, cb: a + jnp.where(s > cb, 1.0, 0.0), 0.0, row_count)
    min_ge = lambda cand: sweep(cand, lambda a, s, cb: jnp.minimum(a, jnp.where(s >= cb, s, inf)),
                                inf, row_min)

    def active(clo, tie):
        return jnp.logical_and(clo > kf, tie == 0.0)

    def any_active(clo, tie):
        return (jnp.max(jnp.where(active(clo, tie), 1.0, 0.0)) > 0.0).astype(I32)

    def cheap_body(st):
        n, _, lo, hi, clo, tie = st
        go = any_active(clo, tie)
        mid = 0.5 * lo + 0.5 * hi
        cand = jnp.where(mid > lo, mid, hi)
        c = count_ge(cand)
        ge = c >= kf
        return (n + 1, go, jnp.where(ge, cand, lo), jnp.where(ge, hi, cand), jnp.where(ge, c, clo), tie)

    def snap(st):
        lo, clo, tie, need = st
        act = active(clo, tie)
        p = min_ge(lo)
        cgt = count_gt(p)
        res = jnp.logical_and(act, cgt < kf)
        return (jnp.where(act, p, lo), clo, jnp.where(res, 1.0, tie), jnp.where(res, kf - cgt, need))

    def outer_body(st):
        _, lo, hi, clo, tie, need = st
        _, _, lo, hi, clo, tie = lax.while_loop(
            lambda s: jnp.logical_and(s[0] < CHEAP_ITERS, s[1] > 0), cheap_body,
            (jnp.int32(0), jnp.int32(1), lo, hi, clo, tie))
        lo, clo, tie, need = lax.cond(any_active(clo, tie) > 0, snap, lambda s: s, (lo, clo, tie, need))
        return any_active(clo, tie), lo, hi, clo, tie, need

    lo0 = jnp.broadcast_to(jnp.min(mn_ref[...], axis=1, keepdims=True), (tq, LANES))
    hi0 = jnp.broadcast_to(jnp.max(mx_ref[...], axis=1, keepdims=True), (tq, LANES))
    clo0 = (qpos + 1).astype(F32)
    zrep = jnp.zeros((tq, LANES), F32)
    _, thr, _, _, tie, need = lax.while_loop(
        lambda s: s[0] > 0, outer_body, (any_active(clo0, zrep), lo0, hi0, clo0, zrep, zrep))

    @pl.when(jnp.max(tie) > 0.0)
    def _():
        rr = lax.broadcasted_iota(I32, (LANES, LANES), 0)
        cc = lax.broadcasted_iota(I32, (LANES, LANES), 1)
        upper = jnp.where(rr <= cc, 1.0, 0.0).astype(BF16)

        def tie_body(j, seen):
            for c in range(nl):
                s = sc_ref[j, :, lanes(c)]
                eqb = (jnp.where(s == thr, 1.0, 0.0) * tie).astype(BF16)
                pref = jnp.dot(eqb, upper, preferred_element_type=F32) + seen
                sc_ref[j, :, lanes(c)] = jnp.where(eqb.astype(F32) * jnp.where(pref > need, 1.0, 0.0) > 0.0,
                                                   -inf, s)
                seen = seen + jnp.dot(eqb, ones_sq, preferred_element_type=F32)
            return seen

        lax.fori_loop(0, nk, tie_body, zrep)

    cb_ref[...] = thr
    q2 = q_ref[...].reshape(rows, HEAD_DIM)
    m_ref[...] = jnp.full(m_ref.shape, -inf, F32)
    acc_ref[...] = jnp.zeros(acc_ref.shape, F32)
    p_ref[1] = jnp.zeros((rows, hk), BF16)

    def qk(start):
        return lax.dot_general(q2, k_ref[pl.ds(start, hk), :], nt_dims, preferred_element_type=F32)

    def pv(slot, start):
        return jnp.dot(p_ref[slot], v_ref[pl.ds(start, hk), :], preferred_element_type=F32)

    def softmax_half(j, slot):
        for c in range(nlh):
            sel = sc_ref[j, :, lanes(slot * nlh + c)] >= cb_ref[...]
            bias_ref[slot, :, lanes(c)] = jnp.where(sel, 0.0, -inf)
        for h in range(N_HEADS):
            mx = None
            for c in range(nlh):
                t = s_ref[slot, head_rows(h), lanes(c)] + bias_ref[slot, :, lanes(c)]
                mx = t if mx is None else jnp.maximum(mx, t)
            m_old = m_ref[h]
            m_new = jnp.maximum(m_old, jnp.max(mx, axis=1, keepdims=True))
            m_safe = jnp.where(m_new == -inf, 0.0, m_new)
            for c in range(nlh):
                t = s_ref[slot, head_rows(h), lanes(c)] + bias_ref[slot, :, lanes(c)]
                p_ref[slot, head_rows(h), lanes(c)] = jnp.exp2(t - m_safe).astype(BF16)
            alpha_ref[slot, head_rows(h), :] = jnp.exp2(m_old - m_safe)
            m_ref[h] = m_new

    def attn_body(j, carry):
        base = pl.multiple_of(j * sk, sk)
        s_ref[0] = qk(base)
        pv_prev = pv(1, pl.multiple_of(jnp.maximum(base - hk, 0), hk))
        softmax_half(j, 0)
        acc_ref[...] = (acc_ref[...] + pv_prev) * alpha_ref[0]
        s_ref[1] = qk(base + hk)
        pv_prev = pv(0, base)
        softmax_half(j, 1)
        acc_ref[...] = (acc_ref[...] + pv_prev) * alpha_ref[1]
        return carry

    lax.fori_loop(0, nk, attn_body, 0)

    acc = acc_ref[...] + pv(1, pl.multiple_of(nk * sk - hk, hk))
    out = acc[:, :HEAD_DIM] / acc[:, HEAD_DIM:HEAD_DIM + 1]
    for h in range(N_HEADS):
        o_ref[:, h * HEAD_DIM:(h + 1) * HEAD_DIM] = out[head_rows(h)].astype(BF16)


def _sparse_attention(q, k, v, iq, ik, iw, tq, sk):
    b, _, l, _ = q.shape
    ksel = min(TOPK_MAX, l // 4)
    rows = N_HEADS * tq
    hk = sk // 2
    whole = lambda n: pl.BlockSpec((None, l, n), lambda bi, i: (bi, 0, 0), pipeline_mode=pl.Buffered(1))
    head = lambda n: pl.BlockSpec((None, N_HEADS, tq, n), lambda bi, i: (bi, 0, i, 0))
    return pl.pallas_call(
        functools.partial(_dsa_kernel, tq=tq, sk=sk, ksel=ksel),
        grid=(b, l // tq),
        in_specs=[head(HEAD_DIM), whole(HEAD_DIM), whole(LANES), head(IDX_DIM), whole(IDX_DIM),
                  pl.BlockSpec((None, tq, IDX_HEADS), lambda bi, i: (bi, i, 0))],
        out_specs=pl.BlockSpec((None, tq, ATTN_W), lambda bi, i: (bi, i, 0)),
        out_shape=jax.ShapeDtypeStruct((b, l, ATTN_W), BF16),
        scratch_shapes=[
            pltpu.VMEM((l // sk, tq, sk), F32),
            pltpu.VMEM((IDX_HEADS, tq, LANES), F32),
            pltpu.VMEM((tq, LANES), F32),
            pltpu.VMEM((tq, LANES), F32),
            pltpu.VMEM((tq, LANES), F32),
            pltpu.VMEM((2, tq, hk), F32),
            pltpu.VMEM((2, rows, hk), F32),
            pltpu.VMEM((2, rows, hk), BF16),
            pltpu.VMEM((N_HEADS, tq, LANES), F32),
            pltpu.VMEM((2, rows, LANES), F32),
            pltpu.VMEM((rows, LANES), F32),
        ],
        compiler_params=_cparams(("arbitrary", "arbitrary")),
        name="sparse_attention",
    )(q, k, v, iq, ik, iw)


def _merge_kernel(x_ref, ca_ref, at_ref, sc1_ref, sh1_ref, gt1_ref, sc2_ref, sh2_ref, g1_ref, g2_ref,
                  wg_ref, wco_ref, bco_ref, wao_ref, wout_ref, wr_ref, br_ref,
                  xo_ref, h2_ref, cmb_ref):
    x = x_ref[...]
    d = x.shape[-1]
    hb = _rms_mod(x, g1_ref[...], sc1_ref[...], sh1_ref[...]).astype(BF16)
    gates = jax.nn.sigmoid(jnp.dot(hb, wg_ref[...], preferred_element_type=F32))
    y_conv = jnp.dot(ca_ref[...], wco_ref[...], preferred_element_type=F32) + bco_ref[...]
    y_attn = jnp.dot(at_ref[...], wao_ref[...], preferred_element_type=F32)
    merged = gates[:, :d] * y_conv + gates[:, d:] * y_attn
    xo = x + gt1_ref[...] * jnp.dot(merged.astype(BF16), wout_ref[...], preferred_element_type=F32)
    xo_ref[...] = xo

    h2 = _rms_mod(xo, g2_ref[...], sc2_ref[...], sh2_ref[...]).astype(BF16)
    h2_ref[...] = h2

    lg = jnp.dot(h2, wr_ref[...], preferred_element_type=F32) + br_ref[...]
    lane = lax.broadcasted_iota(I32, lg.shape, 1)
    big = jnp.int32(LANES)
    is_g = lane < N_GROUPS
    gl = jnp.where(is_g, lg, -jnp.inf)
    gmax = jnp.max(gl, axis=-1, keepdims=True)
    g_sel = jnp.min(jnp.where(gl == gmax, lane, big), axis=-1, keepdims=True)
    p_group = 1.0 / jnp.sum(jnp.where(is_g, jnp.exp(gl - gmax), 0.0), axis=-1, keepdims=True)
    eid = lane - N_GROUPS
    in_grp = jnp.logical_and(jnp.logical_and(eid >= 0, eid < N_EXPERTS),
                             (eid // EXPERTS_PER_GROUP) == g_sel)
    el = jnp.where(in_grp, lg, -jnp.inf)
    v1 = jnp.max(el, axis=-1, keepdims=True)
    i1 = jnp.min(jnp.where(el == v1, lane, big), axis=-1, keepdims=True)
    el2 = jnp.where(lane == i1, -jnp.inf, el)
    v2 = jnp.max(el2, axis=-1, keepdims=True)
    i2 = jnp.min(jnp.where(el2 == v2, lane, big), axis=-1, keepdims=True)
    e21 = jnp.exp(v2 - v1)
    p1 = 1.0 / (1.0 + e21)
    p2 = e21 / (1.0 + e21)
    cmb_ref[...] = p_group * (jnp.where(lane == i1, p1, 0.0) + jnp.where(lane == i2, p2, 0.0))


def _merge(x, conv_act, attn, mods, g1, g2, wts, tm):
    b, l, d = x.shape
    full = lambda a: pl.BlockSpec(a.shape, lambda bi, i: (0,) * a.ndim)
    tok = lambda n: pl.BlockSpec((None, tm, n), lambda bi, i: (bi, i, 0))
    vec = pl.BlockSpec((None, 1, d), lambda bi, i: (bi, 0, 0))
    return pl.pallas_call(
        _merge_kernel,
        grid=(b, l // tm),
        in_specs=[tok(d), tok(CONV_CH), tok(ATTN_W)] + [vec] * 5 + [full(g1), full(g2)]
                 + [full(w) for w in wts],
        out_specs=[tok(d), tok(d), tok(LANES)],
        out_shape=[jax.ShapeDtypeStruct((b, l, d), F32),
                   jax.ShapeDtypeStruct((b, l, d), BF16),
                   jax.ShapeDtypeStruct((b, l, LANES), F32)],
        compiler_params=_cparams(("arbitrary", "arbitrary")),
        name="merge_route",
    )(x, conv_act, attn, *mods, g1, g2, *wts)


def _moe_kernel(h2_ref, cmb_ref, x_ref, gt2_ref, wg_ref, wu_ref, wd_ref, o_ref, acc_ref):
    e = pl.program_id(2)

    @pl.when(e == 0)
    def _():
        acc_ref[...] = jnp.zeros(acc_ref.shape, F32)

    h = h2_ref[...]
    a = jnp.dot(h, wg_ref[...], preferred_element_type=F32)
    u = jnp.dot(h, wu_ref[...], preferred_element_type=F32)
    hid = (a * jax.nn.sigmoid(a)) * u
    y = jnp.dot(hid.astype(BF16), wd_ref[...], preferred_element_type=F32)
    cmb = cmb_ref[...]
    lane = lax.broadcasted_iota(I32, cmb.shape, 1)
    ce = jnp.sum(jnp.where(lane == e + N_GROUPS, cmb, 0.0), axis=-1, keepdims=True)
    acc_ref[...] += ce * y

    @pl.when(e == pl.num_programs(2) - 1)
    def _():
        o_ref[...] = x_ref[...] + gt2_ref[...] * acc_ref[...]


def _moe(h2, cmb, x, gate2, w_gate, w_up, w_down, tm):
    b, l, d = x.shape
    ne, _, de = w_gate.shape
    tok = lambda n: pl.BlockSpec((None, tm, n), lambda bi, i, e: (bi, i, 0))
    return pl.pallas_call(
        _moe_kernel,
        grid=(b, l // tm, ne),
        in_specs=[tok(d), tok(LANES), tok(d),
                  pl.BlockSpec((None, 1, d), lambda bi, i, e: (bi, 0, 0)),
                  pl.BlockSpec((None, d, de), lambda bi, i, e: (e, 0, 0)),
                  pl.BlockSpec((None, d, de), lambda bi, i, e: (e, 0, 0)),
                  pl.BlockSpec((None, de, d), lambda bi, i, e: (e, 0, 0))],
        out_specs=tok(d),
        out_shape=jax.ShapeDtypeStruct((b, l, d), F32),
        scratch_shapes=[pltpu.VMEM((tm, d), F32)],
        compiler_params=_cparams(("arbitrary", "arbitrary", "arbitrary")),
        name="moe_experts",
    )(h2, cmb, x, gate2, w_gate, w_up, w_down)


def _rope_tables(length):
    pos = jnp.arange(length, dtype=F32)

    def cs(dim):
        inv = ROPE_THETA ** (-jnp.arange(0, dim, 2, dtype=F32) / dim)
        ang = pos[:, None] * inv[None, :]
        c, s = jnp.cos(ang), jnp.sin(ang)
        return jnp.concatenate([c, c], -1), jnp.concatenate([-s, s], -1)

    ch, sh = cs(HEAD_DIM)
    ci, si = cs(IDX_DIM)
    pad1 = lambda a: jnp.concatenate([a, jnp.ones((length, LANES - a.shape[1]), F32)], -1)
    pad0 = lambda a: jnp.concatenate([a, jnp.zeros((length, LANES - a.shape[1]), F32)], -1)
    return (jnp.tile(ch, (1, N_HEADS)), jnp.tile(sh, (1, N_HEADS)), pad1(ch), pad0(sh),
            jnp.tile(ci, (1, IDX_HEADS)), jnp.tile(si, (1, IDX_HEADS)), pad1(ci), pad0(si))


def _forward(x, c, w_ada, b_ada, g_norm1, w_in, q_norm_g, k_norm_g, conv_w, conv_b,
             conv_ln_g, conv_ln_b, w_conv_out, b_conv_out, w_attn_out, w_out, g_norm2,
             w_group, b_group, w_router, b_router, w_gate, w_up, w_down,
             *, tm, tl, tq, sk, tmoe):
    b, l, d = x.shape
    depth = w_ada.shape[0]
    tabs = _rope_tables(l)
    ones_bd = jnp.asarray(np.kron(np.eye(N_HEADS), np.ones((HEAD_DIM, HEAD_DIM))), BF16)
    mod = _modulation(c, w_ada, b_ada)
    o_q, o_k, o_iq, o_ik = 2 * CONV_CH, 2 * CONV_CH + ATTN_W, 0, 0
    o_k = o_q + ATTN_W
    o_iq = o_k + 2 * HEAD_DIM
    o_ik = o_iq + IDX_HEADS * IDX_DIM
    o_g = o_ik + IDX_DIM + IDX_HEADS
    for li in range(depth):
        shift1, scale1, gate1, shift2, scale2, gate2 = [
            m.reshape(b, 1, d) for m in jnp.split(mod[li], N_MOD, axis=-1)]
        w = w_in[li]
        wik = jnp.zeros((d, LANES), F32).at[:, :o_g - o_ik].set(w[:, o_ik:o_g])
        in_wts = (w[:, :o_q].astype(BF16), w[:, o_q:o_k].astype(BF16), w[:, o_k:o_iq].astype(BF16),
                  w[:, o_iq:o_ik].astype(BF16), wik.astype(BF16), ones_bd,
                  jnp.tile(q_norm_g[li], N_HEADS).reshape(1, ATTN_W),
                  jnp.concatenate([k_norm_g[li], jnp.ones((LANES - HEAD_DIM,), F32)]).reshape(1, LANES))
        g1 = g_norm1[li].reshape(1, d)
        g2 = g_norm2[li].reshape(1, d)
        uglu, q, k, v, iq, ik, iw = _in_projection(x, scale1, shift1, g1, in_wts, tabs, tm)
        conv_act = _conv_branch(uglu, conv_w[li], conv_b[li], conv_ln_g[li], conv_ln_b[li], tl)
        attn = _sparse_attention(q, k, v, iq, ik, iw, tq, sk)
        wr = jnp.zeros((d, LANES), F32).at[:, :N_GROUPS].set(w_group[li])
        wr = wr.at[:, N_GROUPS:N_GROUPS + N_EXPERTS].set(w_router[li])
        br = jnp.zeros((1, LANES), F32).at[0, :N_GROUPS].set(b_group[li])
        br = br.at[0, N_GROUPS:N_GROUPS + N_EXPERTS].set(b_router[li])
        merge_wts = (w[:, o_g:].astype(BF16), w_conv_out[li].astype(BF16), b_conv_out[li].reshape(1, d),
                     w_attn_out[li].astype(BF16), w_out[li].astype(BF16), wr.astype(BF16), br)
        x, h2, cmb = _merge(x, conv_act, attn, (scale1, shift1, gate1, scale2, shift2), g1, g2,
                            merge_wts, tm)
        x = _moe(h2, cmb, x, gate2, w_gate[li].astype(BF16), w_up[li].astype(BF16),
                 w_down[li].astype(BF16), tmoe)
    return x


def kernel(x, c, w_ada, b_ada, g_norm1, w_in, q_norm_g, k_norm_g, conv_w, conv_b, conv_ln_g, conv_ln_b, w_conv_out, b_conv_out, w_attn_out, w_out, g_norm2, w_group, b_group, w_router, b_router, w_gate, w_up, w_down):
    return _forward(x, c, w_ada, b_ada, g_norm1, w_in, q_norm_g, k_norm_g, conv_w, conv_b,
                    conv_ln_g, conv_ln_b, w_conv_out, b_conv_out, w_attn_out, w_out, g_norm2,
                    w_group, b_group, w_router, b_router, w_gate, w_up, w_down,
                    tm=512, tl=512, tq=512, sk=512, tmoe=1024)
```

```python
import functools

import numpy as np
import jax
import jax.numpy as jnp
from jax import lax
from jax.experimental import pallas as pl
from jax.experimental.pallas import tpu as pltpu

F32 = jnp.float32
BF16 = jnp.bfloat16
I32 = jnp.int32

N_MOD = 6
CONV_CH = 512
CONV_WIDTH = 31
N_HEADS = 8
HEAD_DIM = 64
ATTN_W = N_HEADS * HEAD_DIM
IDX_HEADS = 8
IDX_DIM = 32
TOPK_MAX = 256
ROPE_THETA = 10000.0
INDEX_SCALE = (IDX_DIM ** -0.5) * (IDX_HEADS ** -0.5)
N_GROUPS = 4
EXPERTS_PER_GROUP = 8
N_EXPERTS = N_GROUPS * EXPERTS_PER_GROUP
EPS = 1e-6

LANES = 128
VMEM_LIMIT = 58 * 1024 * 1024
IDX_SUB = 256
SWEEP_ROWS = 128
CHEAP_ITERS = 20
Q_SCALE = (HEAD_DIM ** -0.5) * 1.4426950408889634
GROUP_LANE = N_GROUPS + N_EXPERTS
MOE_SUB = 288
MOE_CHUNK = 256
HALO = 32


def _cparams(sem):
    return pltpu.CompilerParams(dimension_semantics=sem, vmem_limit_bytes=VMEM_LIMIT)


def _rms_mod(x, g, scale, shift):
    ms = jnp.mean(x * x, axis=-1, keepdims=True)
    y = x * lax.rsqrt(ms + EPS) * g
    return y * (1.0 + scale) + shift


def _rot_half(x, half):
    n = x.shape[-1]
    lane = lax.broadcasted_iota(I32, x.shape, x.ndim - 1)
    first = (lane % (2 * half)) < half
    return jnp.where(first, pltpu.roll(x, n - half, x.ndim - 1), pltpu.roll(x, half, x.ndim - 1))


def _mod_kernel(c_ref, w_ref, b_ref, o_ref):
    c = c_ref[...]
    sc = (c * jax.nn.sigmoid(c)).astype(BF16)
    o_ref[...] = jnp.dot(sc, w_ref[...].astype(BF16), preferred_element_type=F32) + b_ref[...]


def _modulation(c, w_ada, b_ada):
    depth, d, n = w_ada.shape
    b = c.shape[0]
    rows = 8
    c_pad = jnp.zeros((rows, d), F32).at[:b].set(c)
    tn = 1536
    out = pl.pallas_call(
        _mod_kernel,
        grid=(depth, n // tn),
        in_specs=[
            pl.BlockSpec((rows, d), lambda l, j: (0, 0)),
            pl.BlockSpec((None, d, tn), lambda l, j: (l, 0, j)),
            pl.BlockSpec((None, 1, tn), lambda l, j: (l, 0, j)),
        ],
        out_specs=pl.BlockSpec((None, rows, tn), lambda l, j: (l, 0, j)),
        out_shape=jax.ShapeDtypeStruct((depth, rows, n), F32),
        compiler_params=_cparams(("arbitrary", "arbitrary")),
        name="adaln_mod",
    )(c_pad, w_ada, b_ada.reshape(depth, 1, n))
    return out[:, :b]


def _inproj_kernel(x_ref, sc_ref, sh_ref, g1_ref, wc_ref, wq_ref, wkv_ref, wiq_ref, wik_ref, ones_ref,
                   qg_ref, kg_ref, cq_ref, sq_ref, ckv_ref, skv_ref, ci_ref, si_ref, cik_ref, sik_ref,
                   uglu_ref, q_ref, k_ref, v_ref, iq_ref, ik_ref, iw_ref):
    hb = _rms_mod(x_ref[...], g1_ref[...], sc_ref[...], sh_ref[...]).astype(BF16)

    uc = jnp.dot(hb, wc_ref[...], preferred_element_type=F32)
    uglu_ref[...] = uc[:, :CONV_CH] * jax.nn.sigmoid(uc[:, CONV_CH:])

    uq = jnp.dot(hb, wq_ref[...], preferred_element_type=F32)
    sq = uq * uq
    sq_hi = sq.astype(BF16)
    sq_lo = (sq - sq_hi.astype(F32)).astype(BF16)
    ssq = (jnp.dot(sq_hi, ones_ref[...], preferred_element_type=F32)
           + jnp.dot(sq_lo, ones_ref[...], preferred_element_type=F32))
    qn = uq * lax.rsqrt(ssq * (1.0 / HEAD_DIM) + EPS) * qg_ref[...]
    qr = (qn * cq_ref[...] + _rot_half(qn, HEAD_DIM // 2) * sq_ref[...]) * Q_SCALE
    for h in range(N_HEADS):
        q_ref[h] = qr[:, h * HEAD_DIM:(h + 1) * HEAD_DIM].astype(BF16)

    ukv = jnp.dot(hb, wkv_ref[...], preferred_element_type=F32)
    lane = lax.broadcasted_iota(I32, ukv.shape, 1)
    is_k = lane < HEAD_DIM
    ssk = jnp.sum(jnp.where(is_k, ukv * ukv, 0.0), axis=-1, keepdims=True)
    kn = ukv * lax.rsqrt(ssk * (1.0 / HEAD_DIM) + EPS) * kg_ref[...]
    kr = kn * ckv_ref[...] + _rot_half(kn, HEAD_DIM // 2) * skv_ref[...]
    k_ref[...] = kr[:, :HEAD_DIM].astype(BF16)
    v_ext = jnp.where(is_k, pltpu.roll(ukv, HEAD_DIM, 1), jnp.where(lane == HEAD_DIM, 1.0, 0.0))
    v_ref[...] = v_ext.astype(BF16)

    uiq = jnp.dot(hb, wiq_ref[...], preferred_element_type=F32)
    iqr = uiq * ci_ref[...] + _rot_half(uiq, IDX_DIM // 2) * si_ref[...]
    for h in range(IDX_HEADS):
        iq_ref[h] = iqr[:, h * IDX_DIM:(h + 1) * IDX_DIM].astype(BF16)

    uik = jnp.dot(hb, wik_ref[...], preferred_element_type=F32)
    ikr = uik * cik_ref[...] + _rot_half(uik, IDX_DIM // 2) * sik_ref[...]
    ik_ref[...] = ikr[:, :IDX_DIM].astype(BF16)
    iw_ref[...] = ikr[:, IDX_DIM:IDX_DIM + IDX_HEADS] * INDEX_SCALE


def _in_projection(x, scale1, shift1, g1, wts, tabs, tm):
    b, l, d = x.shape
    wc, wq, wkv, wiq, wik, ones_bd, qg, kg = wts
    grid = (b, l // tm)
    full = lambda a: pl.BlockSpec(a.shape, lambda bi, i: (0,) * a.ndim)
    tok = lambda n: pl.BlockSpec((None, tm, n), lambda bi, i: (bi, i, 0))
    vec = pl.BlockSpec((None, 1, d), lambda bi, i: (bi, 0, 0))
    tab = lambda a: pl.BlockSpec((tm, a.shape[1]), lambda bi, i: (i, 0))
    head = lambda n: pl.BlockSpec((None, N_HEADS, tm, n), lambda bi, i: (bi, 0, i, 0))
    return pl.pallas_call(
        _inproj_kernel,
        grid=grid,
        in_specs=[tok(d), vec, vec, full(g1), full(wc), full(wq), full(wkv), full(wiq), full(wik),
                  full(ones_bd), full(qg), full(kg)] + [tab(t) for t in tabs],
        out_specs=[tok(CONV_CH), head(HEAD_DIM), tok(HEAD_DIM), tok(LANES), head(IDX_DIM),
                   tok(IDX_DIM), tok(IDX_HEADS)],
        out_shape=[
            jax.ShapeDtypeStruct((b, l, CONV_CH), F32),
            jax.ShapeDtypeStruct((b, N_HEADS, l, HEAD_DIM), BF16),
            jax.ShapeDtypeStruct((b, l, HEAD_DIM), BF16),
            jax.ShapeDtypeStruct((b, l, LANES), BF16),
            jax.ShapeDtypeStruct((b, IDX_HEADS, l, IDX_DIM), BF16),
            jax.ShapeDtypeStruct((b, l, IDX_DIM), BF16),
            jax.ShapeDtypeStruct((b, l, IDX_HEADS), F32),
        ],
        compiler_params=_cparams(("arbitrary", "arbitrary")),
        name="in_projection",
    )(x, scale1, shift1, g1, wc, wq, wkv, wiq, wik, ones_bd, qg, kg, *tabs)


def _conv_kernel(u_ref, halo_ref, w_ref, cb_ref, g_ref, b_ref, o_ref, buf_ref, *, tl, rows):
    i = pl.program_id(1)
    buf_ref[0:HALO] = jnp.where(i > 0, halo_ref[...], 0.0)
    buf_ref[HALO:] = u_ref[...]
    first = HALO - (CONV_WIDTH - 1)
    for r in range(tl // rows):
        acc = jnp.zeros((rows, CONV_CH), F32) + cb_ref[...]
        for j in range(CONV_WIDTH):
            s = r * rows + first + j
            acc = acc + buf_ref[s:s + rows, :] * w_ref[j:j + 1, :]
        mu = jnp.mean(acc, axis=-1, keepdims=True)
        yc = acc - mu
        var = jnp.mean(yc * yc, axis=-1, keepdims=True)
        yn = yc * lax.rsqrt(var + EPS) * g_ref[...] + b_ref[...]
        o_ref[r * rows:(r + 1) * rows, :] = (yn * jax.nn.sigmoid(yn)).astype(BF16)


def _conv_branch(uglu, conv_w, conv_b, ln_g, ln_b, tl):
    b, l, ch = uglu.shape
    per = tl // HALO
    w_pad = jnp.zeros((HALO, ch), F32).at[:CONV_WIDTH].set(conv_w)
    row = lambda a: a.reshape(1, ch)
    full = lambda r: pl.BlockSpec((r, ch), lambda bi, i: (0, 0))
    return pl.pallas_call(
        functools.partial(_conv_kernel, tl=tl, rows=64),
        grid=(b, l // tl),
        in_specs=[
            pl.BlockSpec((None, tl, ch), lambda bi, i: (bi, i, 0)),
            pl.BlockSpec((None, HALO, ch), lambda bi, i: (bi, jnp.maximum(i * per - 1, 0), 0)),
            full(HALO), full(1), full(1), full(1),
        ],
        out_specs=pl.BlockSpec((None, tl, ch), lambda bi, i: (bi, i, 0)),
        out_shape=jax.ShapeDtypeStruct((b, l, ch), BF16),
        scratch_shapes=[pltpu.VMEM((tl + HALO, ch), F32)],
        compiler_params=_cparams(("arbitrary", "arbitrary")),
        name="conformer_conv",
    )(uglu, uglu, w_pad, row(conv_b), row(ln_g), row(ln_b))


def _dsa_kernel(q_ref, k_ref, v_ref, iq_ref, ik_ref, iw_ref, o_ref,
                sc_ref, wb_ref, cb_ref, mn_ref, mx_ref, bias_ref, s_ref, p_ref, m_ref, alpha_ref, acc_ref,
                *, tq, sk, ksel):
    i = pl.program_id(1)
    nk = ((i + 1) * tq + sk - 1) // sk
    nl = sk // LANES
    hk = sk // 2
    nlh = hk // LANES
    rows = N_HEADS * tq
    nt_dims = (((1,), (1,)), ((), ()))
    kf = float(ksel)
    inf = jnp.inf

    def lanes(c):
        return slice(c * LANES, (c + 1) * LANES)

    def head_rows(h):
        return slice(h * tq, (h + 1) * tq)

    iw = iw_ref[...]
    for h in range(IDX_HEADS):
        wb_ref[h] = jnp.broadcast_to(iw[:, h:h + 1], (tq, LANES))
    iq2 = iq_ref[...].reshape(IDX_HEADS * tq, IDX_DIM)
    qpos = i * tq + lax.broadcasted_iota(I32, (tq, LANES), 0)
    lane = lax.broadcasted_iota(I32, (tq, LANES), 1)

    mn_ref[...] = jnp.full((tq, LANES), inf, F32)
    mx_ref[...] = jnp.full((tq, LANES), -inf, F32)

    def score_body(j, carry):
        base = pl.multiple_of(j * sk, sk)
        for c2 in range(sk // IDX_SUB):
            ikc = ik_ref[pl.ds(base + c2 * IDX_SUB, IDX_SUB), :]
            r = lax.dot_general(iq2, ikc, nt_dims, preferred_element_type=F32)
            for half in range(IDX_SUB // LANES):
                c = c2 * (IDX_SUB // LANES) + half
                sc = None
                for h in range(IDX_HEADS):
                    t = jnp.maximum(r[head_rows(h), lanes(half)], 0.0) * wb_ref[h]
                    sc = t if sc is None else sc + t
                mn_ref[...] = jnp.minimum(mn_ref[...], sc)
                mx_ref[...] = jnp.maximum(mx_ref[...], sc)
                kpos = base + c * LANES + lane
                sc_ref[j, :, lanes(c)] = jnp.where(kpos <= qpos, sc, -inf)
        return carry

    lax.fori_loop(0, nk, score_body, 0)

    def sweep(cand, fn, init, reduce):
        accs = []
        for g in range(tq // SWEEP_ROWS):
            rs = slice(g * SWEEP_ROWS, (g + 1) * SWEEP_ROWS)
            cb = cand[rs]

            def body(j, acc, rs=rs, cb=cb):
                for c in range(nl):
                    acc = fn(acc, sc_ref[j, rs, lanes(c)], cb)
                return acc

            accs.append(lax.fori_loop(0, nk, body, jnp.full((SWEEP_ROWS, LANES), init, F32)))
        return jnp.concatenate([reduce(a) for a in accs], axis=0)

    ones_sq = jnp.ones((LANES, LANES), BF16)

    def row_count(acc):
        return jnp.dot(acc.astype(BF16), ones_sq, preferred_element_type=F32)

    def row_min(acc):
        return jnp.broadcast_to(jnp.min(acc, axis=1, keepdims=True), acc.shape)

    count_ge = lambda cand: sweep(cand, lambda a, s, cb: a + jnp.where(s >= cb, 1.0, 0.0), 0.0, row_count)
    count_gt = lambda cand: sweep(cand, lambda a, s, cb: a + jnp.where(s > cb, 1.0, 0.0), 0.0, row_count)
    min_ge = lambda cand: sweep(cand, lambda a, s, cb: jnp.minimum(a, jnp.where(s >= cb, s, inf)),
                                inf, row_min)

    def active(clo, tie):
        return jnp.logical_and(clo > kf, tie == 0.0)

    def any_active(clo, tie):
        return (jnp.max(jnp.where(active(clo, tie), 1.0, 0.0)) > 0.0).astype(I32)

    def cheap_body(st):
        n, _, lo, hi, clo, tie = st
        go = any_active(clo, tie)
        mid = 0.5 * lo + 0.5 * hi
        cand = jnp.where(mid > lo, mid, hi)
        c = count_ge(cand)
        ge = c >= kf
        return (n + 1, go, jnp.where(ge, cand, lo), jnp.where(ge, hi, cand), jnp.where(ge, c, clo), tie)

    def snap(st):
        lo, clo, tie, need = st
        act = active(clo, tie)
        p = min_ge(lo)
        cgt = count_gt(p)
        res = jnp.logical_and(act, cgt < kf)
        return (jnp.where(act, p, lo), clo, jnp.where(res, 1.0, tie), jnp.where(res, kf - cgt, need))

    def outer_body(st):
        _, lo, hi, clo, tie, need = st
        _, _, lo, hi, clo, tie = lax.while_loop(
            lambda s: jnp.logical_and(s[0] < CHEAP_ITERS, s[1] > 0), cheap_body,
            (jnp.int32(0), jnp.int32(1), lo, hi, clo, tie))
        lo, clo, tie, need = lax.cond(any_active(clo, tie) > 0, snap, lambda s: s, (lo, clo, tie, need))
        return any_active(clo, tie), lo, hi, clo, tie, need

    lo0 = jnp.broadcast_to(jnp.min(mn_ref[...], axis=1, keepdims=True), (tq, LANES))
    hi0 = jnp.broadcast_to(jnp.max(mx_ref[...], axis=1, keepdims=True), (tq, LANES))
    clo0 = (qpos + 1).astype(F32)
    zrep = jnp.zeros((tq, LANES), F32)
    _, thr, _, _, tie, need = lax.while_loop(
        lambda s: s[0] > 0, outer_body, (any_active(clo0, zrep), lo0, hi0, clo0, zrep, zrep))

    @pl.when(jnp.max(tie) > 0.0)
    def _():
        rr = lax.broadcasted_iota(I32, (LANES, LANES), 0)
        cc = lax.broadcasted_iota(I32, (LANES, LANES), 1)
        upper = jnp.where(rr <= cc, 1.0, 0.0).astype(BF16)

        def tie_body(j, seen):
            for c in range(nl):
                s = sc_ref[j, :, lanes(c)]
                eqb = (jnp.where(s == thr, 1.0, 0.0) * tie).astype(BF16)
                pref = jnp.dot(eqb, upper, preferred_element_type=F32) + seen
                sc_ref[j, :, lanes(c)] = jnp.where(eqb.astype(F32) * jnp.where(pref > need, 1.0, 0.0) > 0.0,
                                                   -inf, s)
                seen = seen + jnp.dot(eqb, ones_sq, preferred_element_type=F32)
            return seen

        lax.fori_loop(0, nk, tie_body, zrep)

    cb_ref[...] = thr
    q2 = q_ref[...].reshape(rows, HEAD_DIM)
    m_ref[...] = jnp.full(m_ref.shape, -inf, F32)
    acc_ref[...] = jnp.zeros(acc_ref.shape, F32)
    p_ref[1] = jnp.zeros((rows, hk), BF16)

    def qk(start):
        return lax.dot_general(q2, k_ref[pl.ds(start, hk), :], nt_dims, preferred_element_type=F32)

    def pv(slot, start):
        return jnp.dot(p_ref[slot], v_ref[pl.ds(start, hk), :], preferred_element_type=F32)

    def softmax_half(j, slot):
        for c in range(nlh):
            sel = sc_ref[j, :, lanes(slot * nlh + c)] >= cb_ref[...]
            bias_ref[slot, :, lanes(c)] = jnp.where(sel, 0.0, -inf)
        for h in range(N_HEADS):
            mx = None
            for c in range(nlh):
                t = s_ref[slot, head_rows(h), lanes(c)] + bias_ref[slot, :, lanes(c)]
                mx = t if mx is None else jnp.maximum(mx, t)
            m_old = m_ref[h]
            m_new = jnp.maximum(m_old, jnp.max(mx, axis=1, keepdims=True))
            m_safe = jnp.where(m_new == -inf, 0.0, m_new)
            for c in range(nlh):
                t = s_ref[slot, head_rows(h), lanes(c)] + bias_ref[slot, :, lanes(c)]
                p_ref[slot, head_rows(h), lanes(c)] = jnp.exp2(t - m_safe).astype(BF16)
            alpha_ref[slot, head_rows(h), :] = jnp.exp2(m_old - m_safe)
            m_ref[h] = m_new

    def attn_body(j, carry):
        base = pl.multiple_of(j * sk, sk)
        s_ref[0] = qk(base)
        pv_prev = pv(1, pl.multiple_of(jnp.maximum(base - hk, 0), hk))
        softmax_half(j, 0)
        acc_ref[...] = (acc_ref[...] + pv_prev) * alpha_ref[0]
        s_ref[1] = qk(base + hk)
        pv_prev = pv(0, base)
        softmax_half(j, 1)
        acc_ref[...] = (acc_ref[...] + pv_prev) * alpha_ref[1]
        return carry

    lax.fori_loop(0, nk, attn_body, 0)

    acc = acc_ref[...] + pv(1, pl.multiple_of(nk * sk - hk, hk))
    out = acc[:, :HEAD_DIM] / acc[:, HEAD_DIM:HEAD_DIM + 1]
    for h in range(N_HEADS):
        o_ref[:, h * HEAD_DIM:(h + 1) * HEAD_DIM] = out[head_rows(h)].astype(BF16)


def _sparse_attention(q, k, v, iq, ik, iw, tq, sk):
    b, _, l, _ = q.shape
    ksel = min(TOPK_MAX, l // 4)
    rows = N_HEADS * tq
    hk = sk // 2
    whole = lambda n: pl.BlockSpec((None, l, n), lambda bi, i: (bi, 0, 0), pipeline_mode=pl.Buffered(1))
    head = lambda n: pl.BlockSpec((None, N_HEADS, tq, n), lambda bi, i: (bi, 0, i, 0))
    return pl.pallas_call(
        functools.partial(_dsa_kernel, tq=tq, sk=sk, ksel=ksel),
        grid=(b, l // tq),
        in_specs=[head(HEAD_DIM), whole(HEAD_DIM), whole(LANES), head(IDX_DIM), whole(IDX_DIM),
                  pl.BlockSpec((None, tq, IDX_HEADS), lambda bi, i: (bi, i, 0))],
        out_specs=pl.BlockSpec((None, tq, ATTN_W), lambda bi, i: (bi, i, 0)),
        out_shape=jax.ShapeDtypeStruct((b, l, ATTN_W), BF16),
        scratch_shapes=[
            pltpu.VMEM((l // sk, tq, sk), F32),
            pltpu.VMEM((IDX_HEADS, tq, LANES), F32),
            pltpu.VMEM((tq, LANES), F32),
            pltpu.VMEM((tq, LANES), F32),
            pltpu.VMEM((tq, LANES), F32),
            pltpu.VMEM((2, tq, hk), F32),
            pltpu.VMEM((2, rows, hk), F32),
            pltpu.VMEM((2, rows, hk), BF16),
            pltpu.VMEM((N_HEADS, tq, LANES), F32),
            pltpu.VMEM((2, rows, LANES), F32),
            pltpu.VMEM((rows, LANES), F32),
        ],
        compiler_params=_cparams(("arbitrary", "arbitrary")),
        name="sparse_attention",
    )(q, k, v, iq, ik, iw)


def _merge_kernel(x_ref, ca_ref, at_ref, sc1_ref, sh1_ref, gt1_ref, sc2_ref, sh2_ref, g1_ref, g2_ref,
                  wg_ref, wco_ref, bco_ref, wao_ref, wout_ref, wr_ref, br_ref,
                  xo_ref, h2_ref, cmb_ref):
    x = x_ref[...]
    d = x.shape[-1]
    hb = _rms_mod(x, g1_ref[...], sc1_ref[...], sh1_ref[...]).astype(BF16)
    gates = jax.nn.sigmoid(jnp.dot(hb, wg_ref[...], preferred_element_type=F32))
    y_conv = jnp.dot(ca_ref[...], wco_ref[...], preferred_element_type=F32) + bco_ref[...]
    y_attn = jnp.dot(at_ref[...], wao_ref[...], preferred_element_type=F32)
    merged = gates[:, :d] * y_conv + gates[:, d:] * y_attn
    xo = x + gt1_ref[...] * jnp.dot(merged.astype(BF16), wout_ref[...], preferred_element_type=F32)
    xo_ref[...] = xo

    h2 = _rms_mod(xo, g2_ref[...], sc2_ref[...], sh2_ref[...]).astype(BF16)
    h2_ref[...] = h2

    lg = jnp.dot(h2, wr_ref[...], preferred_element_type=F32) + br_ref[...]
    lane = lax.broadcasted_iota(I32, lg.shape, 1)
    big = jnp.int32(LANES)
    is_g = lane < N_GROUPS
    gl = jnp.where(is_g, lg, -jnp.inf)
    gmax = jnp.max(gl, axis=-1, keepdims=True)
    g_sel = jnp.min(jnp.where(gl == gmax, lane, big), axis=-1, keepdims=True)
    p_group = 1.0 / jnp.sum(jnp.where(is_g, jnp.exp(gl - gmax), 0.0), axis=-1, keepdims=True)
    eid = lane - N_GROUPS
    in_grp = jnp.logical_and(jnp.logical_and(eid >= 0, eid < N_EXPERTS),
                             (eid // EXPERTS_PER_GROUP) == g_sel)
    el = jnp.where(in_grp, lg, -jnp.inf)
    v1 = jnp.max(el, axis=-1, keepdims=True)
    i1 = jnp.min(jnp.where(el == v1, lane, big), axis=-1, keepdims=True)
    el2 = jnp.where(lane == i1, -jnp.inf, el)
    v2 = jnp.max(el2, axis=-1, keepdims=True)
    i2 = jnp.min(jnp.where(el2 == v2, lane, big), axis=-1, keepdims=True)
    e21 = jnp.exp(v2 - v1)
    p1 = 1.0 / (1.0 + e21)
    p2 = e21 / (1.0 + e21)
    cmb_ref[...] = (p_group * (jnp.where(lane == i1, p1, 0.0) + jnp.where(lane == i2, p2, 0.0))
                    + jnp.where(lane == GROUP_LANE, g_sel.astype(F32), 0.0))


def _merge(x, conv_act, attn, mods, g1, g2, wts, tm):
    b, l, d = x.shape
    full = lambda a: pl.BlockSpec(a.shape, lambda bi, i: (0,) * a.ndim)
    tok = lambda n: pl.BlockSpec((None, tm, n), lambda bi, i: (bi, i, 0))
    vec = pl.BlockSpec((None, 1, d), lambda bi, i: (bi, 0, 0))
    return pl.pallas_call(
        _merge_kernel,
        grid=(b, l // tm),
        in_specs=[tok(d), tok(CONV_CH), tok(ATTN_W)] + [vec] * 5 + [full(g1), full(g2)]
                 + [full(w) for w in wts],
        out_specs=[tok(d), tok(d), tok(LANES)],
        out_shape=[jax.ShapeDtypeStruct((b, l, d), F32),
                   jax.ShapeDtypeStruct((b, l, d), BF16),
                   jax.ShapeDtypeStruct((b, l, LANES), F32)],
        compiler_params=_cparams(("arbitrary", "arbitrary")),
        name="merge_route",
    )(x, conv_act, attn, *mods, g1, g2, *wts)


def _moe_dispatch_kernel(h2_ref, cmb_ref, ltri_ref, xs_ref, cs_ref, dest_ref, meta_ref):
    tb = h2_ref.shape[0]
    nsub_max = xs_ref.shape[0] // MOE_SUB
    cmb = cmb_ref[...]
    lane = lax.broadcasted_iota(I32, (tb, LANES), 1)
    gcol = cmb[:, GROUP_LANE:GROUP_LANE + 1]
    onehot = jnp.where(lane.astype(F32) == gcol, 1.0, 0.0)
    rank = jnp.dot(ltri_ref[...], onehot.astype(BF16), preferred_element_type=F32)
    n_row = jnp.sum(onehot, axis=0, keepdims=True)
    nsub_row = jnp.zeros_like(n_row)
    for s in range(nsub_max):
        nsub_row = nsub_row + jnp.where(n_row > float(s * MOE_SUB), 1.0, 0.0)
    rr = lax.broadcasted_iota(I32, (LANES, LANES), 0)
    cc = lax.broadcasted_iota(I32, (LANES, LANES), 1)
    before = jnp.where(rr < cc, 1.0, 0.0).astype(BF16)
    nsub8 = jnp.broadcast_to(nsub_row, (8, LANES))
    off8 = jnp.dot(nsub8.astype(BF16), before, preferred_element_type=F32) * float(MOE_SUB)
    dest = jnp.sum(onehot * (off8[0:1] + rank), axis=1, keepdims=True)
    dest_b = jnp.broadcast_to(dest, (tb, LANES))
    dest_ref[...] = dest_b
    total = jnp.sum(nsub8 * jnp.where(lax.broadcasted_iota(I32, (8, LANES), 1) < N_GROUPS, 1.0, 0.0),
                    axis=1, keepdims=True)
    row8 = lax.broadcasted_iota(I32, (8, LANES), 0)
    meta = jnp.where(row8 == 0, off8, jnp.where(row8 == 1, nsub8, jnp.broadcast_to(total, (8, LANES))))
    meta_ref[...] = meta.astype(I32)
    n_used = jnp.max(total).astype(I32)

    dest_t = [jnp.transpose(dest_b[k * LANES:(k + 1) * LANES])[0:1] for k in range(tb // LANES)]
    c1 = cmb.astype(BF16)
    r1 = cmb - c1.astype(F32)
    c2 = r1.astype(BF16)
    c3 = (r1 - c2.astype(F32)).astype(BF16)
    h2 = h2_ref[...]
    sub_row = lax.broadcasted_iota(I32, (MOE_SUB, LANES), 0).astype(F32)

    def fill(s, carry):
        r0 = pl.multiple_of(s * MOE_SUB, 16)
        want = sub_row + (s * MOE_SUB).astype(F32)
        perm = jnp.concatenate([jnp.where(jnp.broadcast_to(dt, (MOE_SUB, LANES)) == want, 1.0, 0.0)
                                for dt in dest_t], axis=1).astype(BF16)
        xs_ref[pl.ds(r0, MOE_SUB), :] = jnp.dot(perm, h2, preferred_element_type=F32).astype(BF16)
        cs_ref[pl.ds(r0, MOE_SUB), :] = (jnp.dot(perm, c1, preferred_element_type=F32)
                                         + jnp.dot(perm, c2, preferred_element_type=F32)
                                         + jnp.dot(perm, c3, preferred_element_type=F32))
        return carry

    def clear(s, carry):
        r0 = pl.multiple_of(s * MOE_SUB, 16)
        xs_ref[pl.ds(r0, MOE_SUB), :] = jnp.zeros((MOE_SUB, xs_ref.shape[1]), BF16)
        cs_ref[pl.ds(r0, MOE_SUB), :] = jnp.zeros((MOE_SUB, LANES), F32)
        return carry

    lax.fori_loop(0, n_used, fill, 0)
    lax.fori_loop(n_used, nsub_max, clear, 0)


def _moe_group_kernel(off_ref, nsub_ref, xs_ref, cs_ref, wg_ref, wu_ref, wd_ref, ys_ref, acc_ref):
    blk = pl.program_id(0)
    e = pl.program_id(1)

    @pl.when(e == 0)
    def _():
        acc_ref[...] = jnp.zeros(acc_ref.shape, F32)

    g = e // EXPERTS_PER_GROUP
    first = off_ref[blk * N_GROUPS + g]
    lane = lax.broadcasted_iota(I32, (MOE_SUB, LANES), 1)

    def body(s, carry):
        r0 = pl.multiple_of(first + s * MOE_SUB, 16)
        x = xs_ref[pl.ds(r0, MOE_SUB), :]
        a = jnp.dot(x, wg_ref[...], preferred_element_type=F32)
        u = jnp.dot(x, wu_ref[...], preferred_element_type=F32)
        hid = (a * jax.nn.sigmoid(a)) * u
        y = jnp.dot(hid.astype(BF16), wd_ref[...], preferred_element_type=F32)
        ce = jnp.sum(jnp.where(lane == e + N_GROUPS, cs_ref[pl.ds(r0, MOE_SUB), :], 0.0),
                     axis=-1, keepdims=True)
        acc_ref[pl.ds(r0, MOE_SUB), :] += ce * y
        return carry

    lax.fori_loop(0, nsub_ref[blk * N_GROUPS + g], body, 0)

    @pl.when(e == pl.num_programs(1) - 1)
    def _():
        ys_ref[...] = acc_ref[...].astype(BF16)


def _moe_combine_kernel(used_ref, ys_ref, dest_ref, x_ref, gt2_ref, o_ref, acc_ref):
    blk = pl.program_id(0)
    tb = x_ref.shape[0]
    acc_ref[...] = jnp.zeros(acc_ref.shape, F32)
    dest_b = dest_ref[...]
    lane = lax.broadcasted_iota(I32, (tb, LANES), 1).astype(F32)
    n_chunks = (used_ref[blk] * MOE_SUB + MOE_CHUNK - 1) // MOE_CHUNK

    def body(c, carry):
        c0 = pl.multiple_of(c * MOE_CHUNK, MOE_CHUNK)
        base = (c * MOE_CHUNK).astype(F32)
        back = jnp.concatenate([jnp.where(dest_b == lane + (base + float(k * LANES)), 1.0, 0.0)
                                for k in range(MOE_CHUNK // LANES)], axis=1).astype(BF16)
        acc_ref[...] += jnp.dot(back, ys_ref[pl.ds(c0, MOE_CHUNK), :], preferred_element_type=F32)
        return carry

    lax.fori_loop(0, n_chunks, body, 0)
    o_ref[...] = x_ref[...] + gt2_ref[...] * acc_ref[...]


def _moe(h2, cmb, x, gate2, w_gate, w_up, w_down, tb):
    b, l, d = x.shape
    ne, _, de = w_gate.shape
    per = l // tb
    nblk = b * per
    rows = MOE_SUB * (-(-(tb + N_GROUPS * (MOE_SUB - 1)) // MOE_SUB))
    rows = -(-rows // MOE_CHUNK) * MOE_CHUNK
    ltri = jnp.asarray(np.tril(np.ones((tb, tb), np.float32), -1), BF16)
    tok = lambda n: pl.BlockSpec((None, tb, n), lambda bi, i: (bi, i, 0))
    srt = lambda n: pl.BlockSpec((None, rows, n), lambda bi, i: (bi * per + i, 0, 0))
    xs, cs, dest, meta = pl.pallas_call(
        _moe_dispatch_kernel,
        grid=(b, per),
        in_specs=[tok(d), tok(LANES), pl.BlockSpec((tb, tb), lambda bi, i: (0, 0))],
        out_specs=[srt(d), srt(LANES), tok(LANES),
                   pl.BlockSpec((None, 8, LANES), lambda bi, i: (bi * per + i, 0, 0))],
        out_shape=[jax.ShapeDtypeStruct((nblk, rows, d), BF16),
                   jax.ShapeDtypeStruct((nblk, rows, LANES), F32),
                   jax.ShapeDtypeStruct((b, l, LANES), F32),
                   jax.ShapeDtypeStruct((nblk, 8, LANES), I32)],
        compiler_params=_cparams(("arbitrary", "arbitrary")),
        name="moe_dispatch",
    )(h2, cmb, ltri)
    first = meta[:, 0, :N_GROUPS].reshape(-1)
    nsub = meta[:, 1, :N_GROUPS].reshape(-1)
    used = meta[:, 2, 0]
    blk3 = lambda n: pl.BlockSpec((None, rows, n), lambda bi, e, *_: (bi, 0, 0))
    wspec = lambda r, c: pl.BlockSpec((None, r, c), lambda bi, e, *_: (e, 0, 0))
    ys = pl.pallas_call(
        _moe_group_kernel,
        grid_spec=pltpu.PrefetchScalarGridSpec(
            num_scalar_prefetch=2,
            grid=(nblk, ne),
            in_specs=[blk3(d), blk3(LANES), wspec(d, de), wspec(d, de), wspec(de, d)],
            out_specs=blk3(d),
            scratch_shapes=[pltpu.VMEM((rows, d), F32)]),
        out_shape=jax.ShapeDtypeStruct((nblk, rows, d), BF16),
        compiler_params=_cparams(("arbitrary", "arbitrary")),
        name="moe_experts",
    )(first, nsub, xs, cs, w_gate, w_up, w_down)
    tok1 = lambda n: pl.BlockSpec((None, tb, n), lambda k, *_: (k // per, k % per, 0))
    return pl.pallas_call(
        _moe_combine_kernel,
        grid_spec=pltpu.PrefetchScalarGridSpec(
            num_scalar_prefetch=1,
            grid=(nblk,),
            in_specs=[pl.BlockSpec((None, rows, d), lambda k, *_: (k, 0, 0)), tok1(LANES), tok1(d),
                      pl.BlockSpec((None, 1, d), lambda k, *_: (k // per, 0, 0))],
            out_specs=tok1(d),
            scratch_shapes=[pltpu.VMEM((tb, d), F32)]),
        out_shape=jax.ShapeDtypeStruct((b, l, d), F32),
        compiler_params=_cparams(("arbitrary",)),
        name="moe_combine",
    )(used, ys, dest, x, gate2)


def _rope_tables(length):
    pos = jnp.arange(length, dtype=F32)

    def cs(dim):
        inv = ROPE_THETA ** (-jnp.arange(0, dim, 2, dtype=F32) / dim)
        ang = pos[:, None] * inv[None, :]
        c, s = jnp.cos(ang), jnp.sin(ang)
        return jnp.concatenate([c, c], -1), jnp.concatenate([-s, s], -1)

    ch, sh = cs(HEAD_DIM)
    ci, si = cs(IDX_DIM)
    pad1 = lambda a: jnp.concatenate([a, jnp.ones((length, LANES - a.shape[1]), F32)], -1)
    pad0 = lambda a: jnp.concatenate([a, jnp.zeros((length, LANES - a.shape[1]), F32)], -1)
    return (jnp.tile(ch, (1, N_HEADS)), jnp.tile(sh, (1, N_HEADS)), pad1(ch), pad0(sh),
            jnp.tile(ci, (1, IDX_HEADS)), jnp.tile(si, (1, IDX_HEADS)), pad1(ci), pad0(si))


def _forward(x, c, w_ada, b_ada, g_norm1, w_in, q_norm_g, k_norm_g, conv_w, conv_b,
             conv_ln_g, conv_ln_b, w_conv_out, b_conv_out, w_attn_out, w_out, g_norm2,
             w_group, b_group, w_router, b_router, w_gate, w_up, w_down,
             *, tm, tl, tq, sk, tmoe):
    b, l, d = x.shape
    depth = w_ada.shape[0]
    tabs = _rope_tables(l)
    ones_bd = jnp.asarray(np.kron(np.eye(N_HEADS), np.ones((HEAD_DIM, HEAD_DIM))), BF16)
    mod = _modulation(c, w_ada, b_ada)
    o_q, o_k, o_iq, o_ik = 2 * CONV_CH, 2 * CONV_CH + ATTN_W, 0, 0
    o_k = o_q + ATTN_W
    o_iq = o_k + 2 * HEAD_DIM
    o_ik = o_iq + IDX_HEADS * IDX_DIM
    o_g = o_ik + IDX_DIM + IDX_HEADS
    for li in range(depth):
        shift1, scale1, gate1, shift2, scale2, gate2 = [
            m.reshape(b, 1, d) for m in jnp.split(mod[li], N_MOD, axis=-1)]
        w = w_in[li]
        wik = jnp.zeros((d, LANES), F32).at[:, :o_g - o_ik].set(w[:, o_ik:o_g])
        in_wts = (w[:, :o_q].astype(BF16), w[:, o_q:o_k].astype(BF16), w[:, o_k:o_iq].astype(BF16),
                  w[:, o_iq:o_ik].astype(BF16), wik.astype(BF16), ones_bd,
                  jnp.tile(q_norm_g[li], N_HEADS).reshape(1, ATTN_W),
                  jnp.concatenate([k_norm_g[li], jnp.ones((LANES - HEAD_DIM,), F32)]).reshape(1, LANES))
        g1 = g_norm1[li].reshape(1, d)
        g2 = g_norm2[li].reshape(1, d)
        uglu, q, k, v, iq, ik, iw = _in_projection(x, scale1, shift1, g1, in_wts, tabs, tm)
        conv_act = _conv_branch(uglu, conv_w[li], conv_b[li], conv_ln_g[li], conv_ln_b[li], tl)
        attn = _sparse_attention(q, k, v, iq, ik, iw, tq, sk)
        wr = jnp.zeros((d, LANES), F32).at[:, :N_GROUPS].set(w_group[li])
        wr = wr.at[:, N_GROUPS:N_GROUPS + N_EXPERTS].set(w_router[li])
        br = jnp.zeros((1, LANES), F32).at[0, :N_GROUPS].set(b_group[li])
        br = br.at[0, N_GROUPS:N_GROUPS + N_EXPERTS].set(b_router[li])
        merge_wts = (w[:, o_g:].astype(BF16), w_conv_out[li].astype(BF16), b_conv_out[li].reshape(1, d),
                     w_attn_out[li].astype(BF16), w_out[li].astype(BF16), wr.astype(BF16), br)
        x, h2, cmb = _merge(x, conv_act, attn, (scale1, shift1, gate1, scale2, shift2), g1, g2,
                            merge_wts, tm)
        x = _moe(h2, cmb, x, gate2, w_gate[li].astype(BF16), w_up[li].astype(BF16),
                 w_down[li].astype(BF16), tmoe)
    return x


def kernel(x, c, w_ada, b_ada, g_norm1, w_in, q_norm_g, k_norm_g, conv_w, conv_b, conv_ln_g, conv_ln_b, w_conv_out, b_conv_out, w_attn_out, w_out, g_norm2, w_group, b_group, w_router, b_router, w_gate, w_up, w_down):
    return _forward(x, c, w_ada, b_ada, g_norm1, w_in, q_norm_g, k_norm_g, conv_w, conv_b,
                    conv_ln_g, conv_ln_b, w_conv_out, b_conv_out, w_attn_out, w_out, g_norm2,
                    w_group, b_group, w_router, b_router, w_gate, w_up, w_down,
                    tm=512, tl=512, tq=512, sk=512, tmoe=1024)
```

```python
import functools

import numpy as np
import jax
import jax.numpy as jnp
from jax import lax
from jax.experimental import pallas as pl
from jax.experimental.pallas import tpu as pltpu

F32 = jnp.float32
BF16 = jnp.bfloat16
I32 = jnp.int32

N_MOD = 6
CONV_CH = 512
CONV_WIDTH = 31
N_HEADS = 8
HEAD_DIM = 64
ATTN_W = N_HEADS * HEAD_DIM
IDX_HEADS = 8
IDX_DIM = 32
TOPK_MAX = 256
ROPE_THETA = 10000.0
INDEX_SCALE = (IDX_DIM ** -0.5) * (IDX_HEADS ** -0.5)
N_GROUPS = 4
EXPERTS_PER_GROUP = 8
N_EXPERTS = N_GROUPS * EXPERTS_PER_GROUP
EPS = 1e-6

LANES = 128
SUBLANES = 8
VMEM_LIMIT = 58 * 1024 * 1024
IDX_SUB = 256
SWEEP_ROWS = 128
CHEAP_ITERS = 20
Q_SCALE = (HEAD_DIM ** -0.5) * 1.4426950408889634
GROUP_LANE = N_GROUPS + N_EXPERTS
MOE_SUB = 288
MOE_EXPERTS_PER_STEP = 4
MOE_CHUNK = 256
HALO = 32


def _cparams(sem):
    return pltpu.CompilerParams(dimension_semantics=sem, vmem_limit_bytes=VMEM_LIMIT)


def _rms_mod(x, g, scale, shift):
    ms = jnp.mean(x * x, axis=-1, keepdims=True)
    y = x * lax.rsqrt(ms + EPS) * g
    return y * (1.0 + scale) + shift


def _rot_half(x, half):
    n = x.shape[-1]
    lane = lax.broadcasted_iota(I32, x.shape, x.ndim - 1)
    first = (lane % (2 * half)) < half
    return jnp.where(first, pltpu.roll(x, n - half, x.ndim - 1), pltpu.roll(x, half, x.ndim - 1))


def _mod_kernel(c_ref, w_ref, b_ref, o_ref):
    c = c_ref[...]
    sc = (c * jax.nn.sigmoid(c)).astype(BF16)
    o_ref[...] = jnp.dot(sc, w_ref[...].astype(BF16), preferred_element_type=F32) + b_ref[...]


def _modulation(c, w_ada, b_ada):
    depth, d, n = w_ada.shape
    b = c.shape[0]
    rows = 8
    c_pad = jnp.zeros((rows, d), F32).at[:b].set(c)
    tn = 1536
    out = pl.pallas_call(
        _mod_kernel,
        grid=(depth, n // tn),
        in_specs=[
            pl.BlockSpec((rows, d), lambda l, j: (0, 0)),
            pl.BlockSpec((None, d, tn), lambda l, j: (l, 0, j)),
            pl.BlockSpec((None, 1, tn), lambda l, j: (l, 0, j)),
        ],
        out_specs=pl.BlockSpec((None, rows, tn), lambda l, j: (l, 0, j)),
        out_shape=jax.ShapeDtypeStruct((depth, rows, n), F32),
        compiler_params=_cparams(("arbitrary", "arbitrary")),
        name="adaln_mod",
    )(c_pad, w_ada, b_ada.reshape(depth, 1, n))
    return out[:, :b]


def _inproj_kernel(x_ref, sc_ref, sh_ref, g1_ref, wc_ref, wq_ref, wkv_ref, wiq_ref, wik_ref, ones_ref,
                   qg_ref, kg_ref, cq_ref, sq_ref, ckv_ref, skv_ref, ci_ref, si_ref, cik_ref, sik_ref,
                   uglu_ref, q_ref, k_ref, v_ref, iq_ref, ik_ref, iw_ref):
    hb = _rms_mod(x_ref[...], g1_ref[...], sc_ref[...], sh_ref[...]).astype(BF16)

    uc = jnp.dot(hb, wc_ref[...], preferred_element_type=F32)
    uglu_ref[...] = uc[:, :CONV_CH] * jax.nn.sigmoid(uc[:, CONV_CH:])

    uq = jnp.dot(hb, wq_ref[...], preferred_element_type=F32)
    sq = uq * uq
    sq_hi = sq.astype(BF16)
    sq_lo = (sq - sq_hi.astype(F32)).astype(BF16)
    ssq = (jnp.dot(sq_hi, ones_ref[...], preferred_element_type=F32)
           + jnp.dot(sq_lo, ones_ref[...], preferred_element_type=F32))
    qn = uq * lax.rsqrt(ssq * (1.0 / HEAD_DIM) + EPS) * qg_ref[...]
    qr = (qn * cq_ref[...] + _rot_half(qn, HEAD_DIM // 2) * sq_ref[...]) * Q_SCALE
    for h in range(N_HEADS):
        q_ref[h] = qr[:, h * HEAD_DIM:(h + 1) * HEAD_DIM].astype(BF16)

    ukv = jnp.dot(hb, wkv_ref[...], preferred_element_type=F32)
    lane = lax.broadcasted_iota(I32, ukv.shape, 1)
    is_k = lane < HEAD_DIM
    ssk = jnp.sum(jnp.where(is_k, ukv * ukv, 0.0), axis=-1, keepdims=True)
    kn = ukv * lax.rsqrt(ssk * (1.0 / HEAD_DIM) + EPS) * kg_ref[...]
    kr = kn * ckv_ref[...] + _rot_half(kn, HEAD_DIM // 2) * skv_ref[...]
    k_ref[...] = kr[:, :HEAD_DIM].astype(BF16)
    v_ext = jnp.where(is_k, pltpu.roll(ukv, HEAD_DIM, 1), jnp.where(lane == HEAD_DIM, 1.0, 0.0))
    v_ref[...] = v_ext.astype(BF16)

    uiq = jnp.dot(hb, wiq_ref[...], preferred_element_type=F32)
    iqr = uiq * ci_ref[...] + _rot_half(uiq, IDX_DIM // 2) * si_ref[...]
    for h in range(IDX_HEADS):
        iq_ref[h] = iqr[:, h * IDX_DIM:(h + 1) * IDX_DIM].astype(BF16)

    uik = jnp.dot(hb, wik_ref[...], preferred_element_type=F32)
    ikr = uik * cik_ref[...] + _rot_half(uik, IDX_DIM // 2) * sik_ref[...]
    ik_ref[...] = ikr[:, :IDX_DIM].astype(BF16)
    iw_ref[...] = ikr[:, IDX_DIM:IDX_DIM + IDX_HEADS] * INDEX_SCALE


def _in_projection(x, scale1, shift1, g1, wts, tabs, tm):
    b, l, d = x.shape
    wc, wq, wkv, wiq, wik, ones_bd, qg, kg = wts
    grid = (b, l // tm)
    full = lambda a: pl.BlockSpec(a.shape, lambda bi, i: (0,) * a.ndim)
    tok = lambda n: pl.BlockSpec((None, tm, n), lambda bi, i: (bi, i, 0))
    vec = pl.BlockSpec((None, 1, d), lambda bi, i: (bi, 0, 0))
    tab = lambda a: pl.BlockSpec((tm, a.shape[1]), lambda bi, i: (i, 0))
    head = lambda n: pl.BlockSpec((None, N_HEADS, tm, n), lambda bi, i: (bi, 0, i, 0))
    return pl.pallas_call(
        _inproj_kernel,
        grid=grid,
        in_specs=[tok(d), vec, vec, full(g1), full(wc), full(wq), full(wkv), full(wiq), full(wik),
                  full(ones_bd), full(qg), full(kg)] + [tab(t) for t in tabs],
        out_specs=[tok(CONV_CH), head(HEAD_DIM), tok(HEAD_DIM), tok(LANES), head(IDX_DIM),
                   tok(IDX_DIM), tok(IDX_HEADS)],
        out_shape=[
            jax.ShapeDtypeStruct((b, l, CONV_CH), F32),
            jax.ShapeDtypeStruct((b, N_HEADS, l, HEAD_DIM), BF16),
            jax.ShapeDtypeStruct((b, l, HEAD_DIM), BF16),
            jax.ShapeDtypeStruct((b, l, LANES), BF16),
            jax.ShapeDtypeStruct((b, IDX_HEADS, l, IDX_DIM), BF16),
            jax.ShapeDtypeStruct((b, l, IDX_DIM), BF16),
            jax.ShapeDtypeStruct((b, l, IDX_HEADS), F32),
        ],
        compiler_params=_cparams(("arbitrary", "arbitrary")),
        name="in_projection",
    )(x, scale1, shift1, g1, wc, wq, wkv, wiq, wik, ones_bd, qg, kg, *tabs)


def _conv_kernel(u_ref, halo_ref, w_ref, cb_ref, g_ref, b_ref, o_ref, buf_ref, *, tl, rows):
    i = pl.program_id(1)
    buf_ref[0, 0:HALO] = jnp.where(i > 0, halo_ref[...], 0.0)
    buf_ref[0, HALO:] = u_ref[...]
    for k in range(1, SUBLANES):
        buf_ref[k, 0:tl + HALO - SUBLANES] = buf_ref[0, k:tl + HALO - SUBLANES + k]
    first = HALO - (CONV_WIDTH - 1)
    for r in range(tl // rows):
        acc = jnp.zeros((rows, CONV_CH), F32) + cb_ref[...]
        for j in range(CONV_WIDTH):
            s = r * rows + first + j
            a = s - s % SUBLANES
            acc = acc + buf_ref[s % SUBLANES, a:a + rows, :] * w_ref[j:j + 1, :]
        mu = jnp.mean(acc, axis=-1, keepdims=True)
        yc = acc - mu
        var = jnp.mean(yc * yc, axis=-1, keepdims=True)
        yn = yc * lax.rsqrt(var + EPS) * g_ref[...] + b_ref[...]
        o_ref[r * rows:(r + 1) * rows, :] = (yn * jax.nn.sigmoid(yn)).astype(BF16)


def _conv_branch(uglu, conv_w, conv_b, ln_g, ln_b, tl):
    b, l, ch = uglu.shape
    per = tl // HALO
    w_pad = jnp.zeros((HALO, ch), F32).at[:CONV_WIDTH].set(conv_w)
    row = lambda a: a.reshape(1, ch)
    full = lambda r: pl.BlockSpec((r, ch), lambda bi, i: (0, 0))
    return pl.pallas_call(
        functools.partial(_conv_kernel, tl=tl, rows=64),
        grid=(b, l // tl),
        in_specs=[
            pl.BlockSpec((None, tl, ch), lambda bi, i: (bi, i, 0)),
            pl.BlockSpec((None, HALO, ch), lambda bi, i: (bi, jnp.maximum(i * per - 1, 0), 0)),
            full(HALO), full(1), full(1), full(1),
        ],
        out_specs=pl.BlockSpec((None, tl, ch), lambda bi, i: (bi, i, 0)),
        out_shape=jax.ShapeDtypeStruct((b, l, ch), BF16),
        scratch_shapes=[pltpu.VMEM((SUBLANES, tl + HALO, ch), F32)],
        compiler_params=_cparams(("arbitrary", "arbitrary")),
        name="conformer_conv",
    )(uglu, uglu, w_pad, row(conv_b), row(ln_g), row(ln_b))


def _dsa_kernel(q_ref, k_ref, v_ref, iq_ref, ik_ref, iw_ref, o_ref,
                sc_ref, wb_ref, cb_ref, mn_ref, mx_ref, bias_ref, s_ref, p_ref, m_ref, alpha_ref, acc_ref,
                *, tq, sk, ksel):
    i = pl.program_id(1)
    nk = ((i + 1) * tq + sk - 1) // sk
    nl = sk // LANES
    hk = sk // 2
    nlh = hk // LANES
    rows = N_HEADS * tq
    nt_dims = (((1,), (1,)), ((), ()))
    kf = float(ksel)
    inf = jnp.inf

    def lanes(c):
        return slice(c * LANES, (c + 1) * LANES)

    def head_rows(h):
        return slice(h * tq, (h + 1) * tq)

    iw = iw_ref[...]
    for h in range(IDX_HEADS):
        wb_ref[h] = jnp.broadcast_to(iw[:, h:h + 1], (tq, LANES))
    iq2 = iq_ref[...].reshape(IDX_HEADS * tq, IDX_DIM)
    qpos = i * tq + lax.broadcasted_iota(I32, (tq, LANES), 0)
    lane = lax.broadcasted_iota(I32, (tq, LANES), 1)

    mn_ref[...] = jnp.full((tq, LANES), inf, F32)
    mx_ref[...] = jnp.full((tq, LANES), -inf, F32)

    def score_body(j, carry):
        base = pl.multiple_of(j * sk, sk)
        for c2 in range(sk // IDX_SUB):
            ikc = ik_ref[pl.ds(base + c2 * IDX_SUB, IDX_SUB), :]
            r = lax.dot_general(iq2, ikc, nt_dims, preferred_element_type=F32)
            for half in range(IDX_SUB // LANES):
                c = c2 * (IDX_SUB // LANES) + half
                sc = None
                for h in range(IDX_HEADS):
                    t = jnp.maximum(r[head_rows(h), lanes(half)], 0.0) * wb_ref[h]
                    sc = t if sc is None else sc + t
                mn_ref[...] = jnp.minimum(mn_ref[...], sc)
                mx_ref[...] = jnp.maximum(mx_ref[...], sc)
                kpos = base + c * LANES + lane
                sc_ref[j, :, lanes(c)] = jnp.where(kpos <= qpos, sc, -inf)
        return carry

    lax.fori_loop(0, nk, score_body, 0)

    def sweep(cand, fn, init, reduce):
        accs = []
        for g in range(tq // SWEEP_ROWS):
            rs = slice(g * SWEEP_ROWS, (g + 1) * SWEEP_ROWS)
            cb = cand[rs]

            def body(j, acc, rs=rs, cb=cb):
                for c in range(nl):
                    acc = fn(acc, sc_ref[j, rs, lanes(c)], cb)
                return acc

            accs.append(lax.fori_loop(0, nk, body, jnp.full((SWEEP_ROWS, LANES), init, F32)))
        return jnp.concatenate([reduce(a) for a in accs], axis=0)

    ones_sq = jnp.ones((LANES, LANES), BF16)

    def row_count(acc):
        return jnp.dot(acc.astype(BF16), ones_sq, preferred_element_type=F32)

    def row_min(acc):
        return jnp.broadcast_to(jnp.min(acc, axis=1, keepdims=True), acc.shape)

    count_ge = lambda cand: sweep(cand, lambda a, s, cb: a + jnp.where(s >= cb, 1.0, 0.0), 0.0, row_count)
    count_gt = lambda cand: sweep(cand, lambda a, s, cb: a + jnp.where(s > cb, 1.0, 0.0), 0.0, row_count)
    min_ge = lambda cand: sweep(cand, lambda a, s, cb: jnp.minimum(a, jnp.where(s >= cb, s, inf)),
                                inf, row_min)

    def active(clo, tie):
        return jnp.logical_and(clo > kf, tie == 0.0)

    def any_active(clo, tie):
        return (jnp.max(jnp.where(active(clo, tie), 1.0, 0.0)) > 0.0).astype(I32)

    def cheap_body(st):
        n, _, lo, hi, clo, tie = st
        go = any_active(clo, tie)
        mid = 0.5 * lo + 0.5 * hi
        cand = jnp.where(mid > lo, mid, hi)
        c = count_ge(cand)
        ge = c >= kf
        return (n + 1, go, jnp.where(ge, cand, lo), jnp.where(ge, hi, cand), jnp.where(ge, c, clo), tie)

    def snap(st):
        lo, clo, tie, need = st
        act = active(clo, tie)
        p = min_ge(lo)
        cgt = count_gt(p)
        res = jnp.logical_and(act, cgt < kf)
        return (jnp.where(act, p, lo), clo, jnp.where(res, 1.0, tie), jnp.where(res, kf - cgt, need))

    def outer_body(st):
        _, lo, hi, clo, tie, need = st
        _, _, lo, hi, clo, tie = lax.while_loop(
            lambda s: jnp.logical_and(s[0] < CHEAP_ITERS, s[1] > 0), cheap_body,
            (jnp.int32(0), jnp.int32(1), lo, hi, clo, tie))
        lo, clo, tie, need = lax.cond(any_active(clo, tie) > 0, snap, lambda s: s, (lo, clo, tie, need))
        return any_active(clo, tie), lo, hi, clo, tie, need

    lo0 = jnp.broadcast_to(jnp.min(mn_ref[...], axis=1, keepdims=True), (tq, LANES))
    hi0 = jnp.broadcast_to(jnp.max(mx_ref[...], axis=1, keepdims=True), (tq, LANES))
    clo0 = (qpos + 1).astype(F32)
    zrep = jnp.zeros((tq, LANES), F32)
    _, thr, _, _, tie, need = lax.while_loop(
        lambda s: s[0] > 0, outer_body, (any_active(clo0, zrep), lo0, hi0, clo0, zrep, zrep))

    @pl.when(jnp.max(tie) > 0.0)
    def _():
        rr = lax.broadcasted_iota(I32, (LANES, LANES), 0)
        cc = lax.broadcasted_iota(I32, (LANES, LANES), 1)
        upper = jnp.where(rr <= cc, 1.0, 0.0).astype(BF16)

        def tie_body(j, seen):
            for c in range(nl):
                s = sc_ref[j, :, lanes(c)]
                eqb = (jnp.where(s == thr, 1.0, 0.0) * tie).astype(BF16)
                pref = jnp.dot(eqb, upper, preferred_element_type=F32) + seen
                sc_ref[j, :, lanes(c)] = jnp.where(eqb.astype(F32) * jnp.where(pref > need, 1.0, 0.0) > 0.0,
                                                   -inf, s)
                seen = seen + jnp.dot(eqb, ones_sq, preferred_element_type=F32)
            return seen

        lax.fori_loop(0, nk, tie_body, zrep)

    cb_ref[...] = thr
    q2 = q_ref[...].reshape(rows, HEAD_DIM)
    m_ref[...] = jnp.full(m_ref.shape, -inf, F32)
    acc_ref[...] = jnp.zeros(acc_ref.shape, F32)
    p_ref[1] = jnp.zeros((rows, hk), BF16)

    def qk(start):
        return lax.dot_general(q2, k_ref[pl.ds(start, hk), :], nt_dims, preferred_element_type=F32)

    def pv(slot, start):
        return jnp.dot(p_ref[slot], v_ref[pl.ds(start, hk), :], preferred_element_type=F32)

    def softmax_half(j, slot):
        for c in range(nlh):
            sel = sc_ref[j, :, lanes(slot * nlh + c)] >= cb_ref[...]
            bias_ref[slot, :, lanes(c)] = jnp.where(sel, 0.0, -inf)
        for h in range(N_HEADS):
            mx = None
            for c in range(nlh):
                t = s_ref[slot, head_rows(h), lanes(c)] + bias_ref[slot, :, lanes(c)]
                mx = t if mx is None else jnp.maximum(mx, t)
            m_old = m_ref[h]
            m_new = jnp.maximum(m_old, jnp.max(mx, axis=1, keepdims=True))
            m_safe = jnp.where(m_new == -inf, 0.0, m_new)
            for c in range(nlh):
                t = s_ref[slot, head_rows(h), lanes(c)] + bias_ref[slot, :, lanes(c)]
                p_ref[slot, head_rows(h), lanes(c)] = jnp.exp2(t - m_safe).astype(BF16)
            alpha_ref[slot, head_rows(h), :] = jnp.exp2(m_old - m_safe)
            m_ref[h] = m_new

    def attn_body(j, carry):
        base = pl.multiple_of(j * sk, sk)
        s_ref[0] = qk(base)
        pv_prev = pv(1, pl.multiple_of(jnp.maximum(base - hk, 0), hk))
        softmax_half(j, 0)
        acc_ref[...] = (acc_ref[...] + pv_prev) * alpha_ref[0]
        s_ref[1] = qk(base + hk)
        pv_prev = pv(0, base)
        softmax_half(j, 1)
        acc_ref[...] = (acc_ref[...] + pv_prev) * alpha_ref[1]
        return carry

    lax.fori_loop(0, nk, attn_body, 0)

    acc = acc_ref[...] + pv(1, pl.multiple_of(nk * sk - hk, hk))
    out = acc[:, :HEAD_DIM] / acc[:, HEAD_DIM:HEAD_DIM + 1]
    for h in range(N_HEADS):
        o_ref[:, h * HEAD_DIM:(h + 1) * HEAD_DIM] = out[head_rows(h)].astype(BF16)


def _sparse_attention(q, k, v, iq, ik, iw, tq, sk):
    b, _, l, _ = q.shape
    ksel = min(TOPK_MAX, l // 4)
    rows = N_HEADS * tq
    hk = sk // 2
    whole = lambda n: pl.BlockSpec((None, l, n), lambda bi, i: (bi, 0, 0), pipeline_mode=pl.Buffered(1))
    head = lambda n: pl.BlockSpec((None, N_HEADS, tq, n), lambda bi, i: (bi, 0, i, 0))
    return pl.pallas_call(
        functools.partial(_dsa_kernel, tq=tq, sk=sk, ksel=ksel),
        grid=(b, l // tq),
        in_specs=[head(HEAD_DIM), whole(HEAD_DIM), whole(LANES), head(IDX_DIM), whole(IDX_DIM),
                  pl.BlockSpec((None, tq, IDX_HEADS), lambda bi, i: (bi, i, 0))],
        out_specs=pl.BlockSpec((None, tq, ATTN_W), lambda bi, i: (bi, i, 0)),
        out_shape=jax.ShapeDtypeStruct((b, l, ATTN_W), BF16),
        scratch_shapes=[
            pltpu.VMEM((l // sk, tq, sk), F32),
            pltpu.VMEM((IDX_HEADS, tq, LANES), F32),
            pltpu.VMEM((tq, LANES), F32),
            pltpu.VMEM((tq, LANES), F32),
            pltpu.VMEM((tq, LANES), F32),
            pltpu.VMEM((2, tq, hk), F32),
            pltpu.VMEM((2, rows, hk), F32),
            pltpu.VMEM((2, rows, hk), BF16),
            pltpu.VMEM((N_HEADS, tq, LANES), F32),
            pltpu.VMEM((2, rows, LANES), F32),
            pltpu.VMEM((rows, LANES), F32),
        ],
        compiler_params=_cparams(("arbitrary", "arbitrary")),
        name="sparse_attention",
    )(q, k, v, iq, ik, iw)


def _merge_kernel(x_ref, ca_ref, at_ref, sc1_ref, sh1_ref, gt1_ref, sc2_ref, sh2_ref, g1_ref, g2_ref,
                  wg_ref, wco_ref, bco_ref, wao_ref, wout_ref, wr_ref, br_ref,
                  xo_ref, h2_ref, cmb_ref):
    x = x_ref[...]
    d = x.shape[-1]
    hb = _rms_mod(x, g1_ref[...], sc1_ref[...], sh1_ref[...]).astype(BF16)
    gates = jax.nn.sigmoid(jnp.dot(hb, wg_ref[...], preferred_element_type=F32))
    y_conv = jnp.dot(ca_ref[...], wco_ref[...], preferred_element_type=F32) + bco_ref[...]
    y_attn = jnp.dot(at_ref[...], wao_ref[...], preferred_element_type=F32)
    merged = gates[:, :d] * y_conv + gates[:, d:] * y_attn
    xo = x + gt1_ref[...] * jnp.dot(merged.astype(BF16), wout_ref[...], preferred_element_type=F32)
    xo_ref[...] = xo

    h2 = _rms_mod(xo, g2_ref[...], sc2_ref[...], sh2_ref[...]).astype(BF16)
    h2_ref[...] = h2

    lg = jnp.dot(h2, wr_ref[...], preferred_element_type=F32) + br_ref[...]
    lane = lax.broadcasted_iota(I32, lg.shape, 1)
    big = jnp.int32(LANES)
    is_g = lane < N_GROUPS
    gl = jnp.where(is_g, lg, -jnp.inf)
    gmax = jnp.max(gl, axis=-1, keepdims=True)
    g_sel = jnp.min(jnp.where(gl == gmax, lane, big), axis=-1, keepdims=True)
    p_group = 1.0 / jnp.sum(jnp.where(is_g, jnp.exp(gl - gmax), 0.0), axis=-1, keepdims=True)
    eid = lane - N_GROUPS
    in_grp = jnp.logical_and(jnp.logical_and(eid >= 0, eid < N_EXPERTS),
                             (eid // EXPERTS_PER_GROUP) == g_sel)
    el = jnp.where(in_grp, lg, -jnp.inf)
    v1 = jnp.max(el, axis=-1, keepdims=True)
    i1 = jnp.min(jnp.where(el == v1, lane, big), axis=-1, keepdims=True)
    el2 = jnp.where(lane == i1, -jnp.inf, el)
    v2 = jnp.max(el2, axis=-1, keepdims=True)
    i2 = jnp.min(jnp.where(el2 == v2, lane, big), axis=-1, keepdims=True)
    e21 = jnp.exp(v2 - v1)
    p1 = 1.0 / (1.0 + e21)
    p2 = e21 / (1.0 + e21)
    cmb_ref[...] = (p_group * (jnp.where(lane == i1, p1, 0.0) + jnp.where(lane == i2, p2, 0.0))
                    + jnp.where(lane == GROUP_LANE, g_sel.astype(F32), 0.0))


def _merge(x, conv_act, attn, mods, g1, g2, wts, tm):
    b, l, d = x.shape
    full = lambda a: pl.BlockSpec(a.shape, lambda bi, i: (0,) * a.ndim)
    tok = lambda n: pl.BlockSpec((None, tm, n), lambda bi, i: (bi, i, 0))
    vec = pl.BlockSpec((None, 1, d), lambda bi, i: (bi, 0, 0))
    return pl.pallas_call(
        _merge_kernel,
        grid=(b, l // tm),
        in_specs=[tok(d), tok(CONV_CH), tok(ATTN_W)] + [vec] * 5 + [full(g1), full(g2)]
                 + [full(w) for w in wts],
        out_specs=[tok(d), tok(d), tok(LANES)],
        out_shape=[jax.ShapeDtypeStruct((b, l, d), F32),
                   jax.ShapeDtypeStruct((b, l, d), BF16),
                   jax.ShapeDtypeStruct((b, l, LANES), F32)],
        compiler_params=_cparams(("arbitrary", "arbitrary")),
        name="merge_route",
    )(x, conv_act, attn, *mods, g1, g2, *wts)


def _moe_dispatch_kernel(h2_ref, cmb_ref, ltri_ref, xs_ref, cs_ref, dest_ref, meta_ref):
    tb = h2_ref.shape[0]
    nsub_max = xs_ref.shape[0] // MOE_SUB
    cmb = cmb_ref[...]
    lane = lax.broadcasted_iota(I32, (tb, LANES), 1)
    gcol = cmb[:, GROUP_LANE:GROUP_LANE + 1]
    onehot = jnp.where(lane.astype(F32) == gcol, 1.0, 0.0)
    rank = jnp.dot(ltri_ref[...], onehot.astype(BF16), preferred_element_type=F32)
    n_row = jnp.sum(onehot, axis=0, keepdims=True)
    nsub_row = jnp.zeros_like(n_row)
    for s in range(nsub_max):
        nsub_row = nsub_row + jnp.where(n_row > float(s * MOE_SUB), 1.0, 0.0)
    rr = lax.broadcasted_iota(I32, (LANES, LANES), 0)
    cc = lax.broadcasted_iota(I32, (LANES, LANES), 1)
    before = jnp.where(rr < cc, 1.0, 0.0).astype(BF16)
    nsub8 = jnp.broadcast_to(nsub_row, (8, LANES))
    off8 = jnp.dot(nsub8.astype(BF16), before, preferred_element_type=F32) * float(MOE_SUB)
    dest = jnp.sum(onehot * (off8[0:1] + rank), axis=1, keepdims=True)
    dest_b = jnp.broadcast_to(dest, (tb, LANES))
    dest_ref[...] = dest_b
    total = jnp.sum(nsub8 * jnp.where(lax.broadcasted_iota(I32, (8, LANES), 1) < N_GROUPS, 1.0, 0.0),
                    axis=1, keepdims=True)
    row8 = lax.broadcasted_iota(I32, (8, LANES), 0)
    meta = jnp.where(row8 == 0, off8, jnp.where(row8 == 1, nsub8, jnp.broadcast_to(total, (8, LANES))))
    meta_ref[...] = meta.astype(I32)
    n_used = jnp.max(total).astype(I32)

    dest_t = [jnp.transpose(dest_b[k * LANES:(k + 1) * LANES])[0:1] for k in range(tb // LANES)]
    c1 = cmb.astype(BF16)
    r1 = cmb - c1.astype(F32)
    c2 = r1.astype(BF16)
    c3 = (r1 - c2.astype(F32)).astype(BF16)
    h2 = h2_ref[...]
    sub_row = lax.broadcasted_iota(I32, (MOE_SUB, LANES), 0).astype(F32)

    def fill(s, carry):
        r0 = pl.multiple_of(s * MOE_SUB, 16)
        want = sub_row + (s * MOE_SUB).astype(F32)
        perm = jnp.concatenate([jnp.where(jnp.broadcast_to(dt, (MOE_SUB, LANES)) == want, 1.0, 0.0)
                                for dt in dest_t], axis=1).astype(BF16)
        xs_ref[pl.ds(r0, MOE_SUB), :] = jnp.dot(perm, h2, preferred_element_type=F32).astype(BF16)
        cs_ref[pl.ds(r0, MOE_SUB), :] = (jnp.dot(perm, c1, preferred_element_type=F32)
                                         + jnp.dot(perm, c2, preferred_element_type=F32)
                                         + jnp.dot(perm, c3, preferred_element_type=F32))
        return carry

    def clear(s, carry):
        r0 = pl.multiple_of(s * MOE_SUB, 16)
        xs_ref[pl.ds(r0, MOE_SUB), :] = jnp.zeros((MOE_SUB, xs_ref.shape[1]), BF16)
        cs_ref[pl.ds(r0, MOE_SUB), :] = jnp.zeros((MOE_SUB, LANES), F32)
        return carry

    lax.fori_loop(0, n_used, fill, 0)
    lax.fori_loop(n_used, nsub_max, clear, 0)


def _moe_group_kernel(off_ref, nsub_ref, xs_ref, cs_ref, wg_ref, wu_ref, wd_ref, ys_ref, acc_ref):
    blk = pl.program_id(0)
    step = pl.program_id(1)
    per_step = wg_ref.shape[0]

    @pl.when(step == 0)
    def _():
        acc_ref[...] = jnp.zeros(acc_ref.shape, F32)

    g = (step * per_step) // EXPERTS_PER_GROUP
    first = off_ref[blk * N_GROUPS + g]
    lane = lax.broadcasted_iota(I32, (MOE_SUB, LANES), 1)

    def body(s, carry):
        r0 = pl.multiple_of(first + s * MOE_SUB, 16)
        x = xs_ref[pl.ds(r0, MOE_SUB), :]
        cmb = cs_ref[pl.ds(r0, MOE_SUB), :]
        out = None
        for k in range(per_step):
            a = jnp.dot(x, wg_ref[k], preferred_element_type=F32)
            u = jnp.dot(x, wu_ref[k], preferred_element_type=F32)
            hid = (a * jax.nn.sigmoid(a)) * u
            y = jnp.dot(hid.astype(BF16), wd_ref[k], preferred_element_type=F32)
            ce = jnp.sum(jnp.where(lane == step * per_step + k + N_GROUPS, cmb, 0.0),
                         axis=-1, keepdims=True)
            out = ce * y if out is None else out + ce * y
        acc_ref[pl.ds(r0, MOE_SUB), :] += out
        return carry

    lax.fori_loop(0, nsub_ref[blk * N_GROUPS + g], body, 0)

    @pl.when(step == pl.num_programs(1) - 1)
    def _():
        ys_ref[...] = acc_ref[...].astype(BF16)


def _moe_combine_kernel(used_ref, ys_ref, dest_ref, x_ref, gt2_ref, o_ref, acc_ref):
    blk = pl.program_id(0)
    tb = x_ref.shape[0]
    acc_ref[...] = jnp.zeros(acc_ref.shape, F32)
    dest_b = dest_ref[...]
    lane = lax.broadcasted_iota(I32, (tb, LANES), 1).astype(F32)
    n_chunks = (used_ref[blk] * MOE_SUB + MOE_CHUNK - 1) // MOE_CHUNK

    def body(c, carry):
        c0 = pl.multiple_of(c * MOE_CHUNK, MOE_CHUNK)
        base = (c * MOE_CHUNK).astype(F32)
        back = jnp.concatenate([jnp.where(dest_b == lane + (base + float(k * LANES)), 1.0, 0.0)
                                for k in range(MOE_CHUNK // LANES)], axis=1).astype(BF16)
        acc_ref[...] += jnp.dot(back, ys_ref[pl.ds(c0, MOE_CHUNK), :], preferred_element_type=F32)
        return carry

    lax.fori_loop(0, n_chunks, body, 0)
    o_ref[...] = x_ref[...] + gt2_ref[...] * acc_ref[...]


def _moe(h2, cmb, x, gate2, w_gate, w_up, w_down, tb):
    b, l, d = x.shape
    ne, _, de = w_gate.shape
    per = l // tb
    nblk = b * per
    rows = MOE_SUB * (-(-(tb + N_GROUPS * (MOE_SUB - 1)) // MOE_SUB))
    rows = -(-rows // MOE_CHUNK) * MOE_CHUNK
    ltri = jnp.asarray(np.tril(np.ones((tb, tb), np.float32), -1), BF16)
    tok = lambda n: pl.BlockSpec((None, tb, n), lambda bi, i: (bi, i, 0))
    srt = lambda n: pl.BlockSpec((None, rows, n), lambda bi, i: (bi * per + i, 0, 0))
    xs, cs, dest, meta = pl.pallas_call(
        _moe_dispatch_kernel,
        grid=(b, per),
        in_specs=[tok(d), tok(LANES), pl.BlockSpec((tb, tb), lambda bi, i: (0, 0))],
        out_specs=[srt(d), srt(LANES), tok(LANES),
                   pl.BlockSpec((None, 8, LANES), lambda bi, i: (bi * per + i, 0, 0))],
        out_shape=[jax.ShapeDtypeStruct((nblk, rows, d), BF16),
                   jax.ShapeDtypeStruct((nblk, rows, LANES), F32),
                   jax.ShapeDtypeStruct((b, l, LANES), F32),
                   jax.ShapeDtypeStruct((nblk, 8, LANES), I32)],
        compiler_params=_cparams(("arbitrary", "arbitrary")),
        name="moe_dispatch",
    )(h2, cmb, ltri)
    first = meta[:, 0, :N_GROUPS].reshape(-1)
    nsub = meta[:, 1, :N_GROUPS].reshape(-1)
    used = meta[:, 2, 0]
    blk3 = lambda n: pl.BlockSpec((None, rows, n), lambda bi, e, *_: (bi, 0, 0))
    wspec = lambda r, c: pl.BlockSpec((MOE_EXPERTS_PER_STEP, r, c), lambda bi, e, *_: (e, 0, 0))
    ys = pl.pallas_call(
        _moe_group_kernel,
        grid_spec=pltpu.PrefetchScalarGridSpec(
            num_scalar_prefetch=2,
            grid=(nblk, ne // MOE_EXPERTS_PER_STEP),
            in_specs=[blk3(d), blk3(LANES), wspec(d, de), wspec(d, de), wspec(de, d)],
            out_specs=blk3(d),
            scratch_shapes=[pltpu.VMEM((rows, d), F32)]),
        out_shape=jax.ShapeDtypeStruct((nblk, rows, d), BF16),
        compiler_params=_cparams(("arbitrary", "arbitrary")),
        name="moe_experts",
    )(first, nsub, xs, cs, w_gate, w_up, w_down)
    tok1 = lambda n: pl.BlockSpec((None, tb, n), lambda k, *_: (k // per, k % per, 0))
    return pl.pallas_call(
        _moe_combine_kernel,
        grid_spec=pltpu.PrefetchScalarGridSpec(
            num_scalar_prefetch=1,
            grid=(nblk,),
            in_specs=[pl.BlockSpec((None, rows, d), lambda k, *_: (k, 0, 0)), tok1(LANES), tok1(d),
                      pl.BlockSpec((None, 1, d), lambda k, *_: (k // per, 0, 0))],
            out_specs=tok1(d),
            scratch_shapes=[pltpu.VMEM((tb, d), F32)]),
        out_shape=jax.ShapeDtypeStruct((b, l, d), F32),
        compiler_params=_cparams(("arbitrary",)),
        name="moe_combine",
    )(used, ys, dest, x, gate2)


def _rope_tables(length):
    pos = jnp.arange(length, dtype=F32)

    def cs(dim):
        inv = ROPE_THETA ** (-jnp.arange(0, dim, 2, dtype=F32) / dim)
        ang = pos[:, None] * inv[None, :]
        c, s = jnp.cos(ang), jnp.sin(ang)
        return jnp.concatenate([c, c], -1), jnp.concatenate([-s, s], -1)

    ch, sh = cs(HEAD_DIM)
    ci, si = cs(IDX_DIM)
    pad1 = lambda a: jnp.concatenate([a, jnp.ones((length, LANES - a.shape[1]), F32)], -1)
    pad0 = lambda a: jnp.concatenate([a, jnp.zeros((length, LANES - a.shape[1]), F32)], -1)
    return (jnp.tile(ch, (1, N_HEADS)), jnp.tile(sh, (1, N_HEADS)), pad1(ch), pad0(sh),
            jnp.tile(ci, (1, IDX_HEADS)), jnp.tile(si, (1, IDX_HEADS)), pad1(ci), pad0(si))


def _forward(x, c, w_ada, b_ada, g_norm1, w_in, q_norm_g, k_norm_g, conv_w, conv_b,
             conv_ln_g, conv_ln_b, w_conv_out, b_conv_out, w_attn_out, w_out, g_norm2,
             w_group, b_group, w_router, b_router, w_gate, w_up, w_down,
             *, tm, tl, tq, sk, tmoe):
    b, l, d = x.shape
    depth = w_ada.shape[0]
    tabs = _rope_tables(l)
    ones_bd = jnp.asarray(np.kron(np.eye(N_HEADS), np.ones((HEAD_DIM, HEAD_DIM))), BF16)
    mod = _modulation(c, w_ada, b_ada)
    o_q, o_k, o_iq, o_ik = 2 * CONV_CH, 2 * CONV_CH + ATTN_W, 0, 0
    o_k = o_q + ATTN_W
    o_iq = o_k + 2 * HEAD_DIM
    o_ik = o_iq + IDX_HEADS * IDX_DIM
    o_g = o_ik + IDX_DIM + IDX_HEADS
    for li in range(depth):
        shift1, scale1, gate1, shift2, scale2, gate2 = [
            m.reshape(b, 1, d) for m in jnp.split(mod[li], N_MOD, axis=-1)]
        w = w_in[li]
        wik = jnp.zeros((d, LANES), F32).at[:, :o_g - o_ik].set(w[:, o_ik:o_g])
        in_wts = (w[:, :o_q].astype(BF16), w[:, o_q:o_k].astype(BF16), w[:, o_k:o_iq].astype(BF16),
                  w[:, o_iq:o_ik].astype(BF16), wik.astype(BF16), ones_bd,
                  jnp.tile(q_norm_g[li], N_HEADS).reshape(1, ATTN_W),
                  jnp.concatenate([k_norm_g[li], jnp.ones((LANES - HEAD_DIM,), F32)]).reshape(1, LANES))
        g1 = g_norm1[li].reshape(1, d)
        g2 = g_norm2[li].reshape(1, d)
        uglu, q, k, v, iq, ik, iw = _in_projection(x, scale1, shift1, g1, in_wts, tabs, tm)
        conv_act = _conv_branch(uglu, conv_w[li], conv_b[li], conv_ln_g[li], conv_ln_b[li], tl)
        attn = _sparse_attention(q, k, v, iq, ik, iw, tq, sk)
        wr = jnp.zeros((d, LANES), F32).at[:, :N_GROUPS].set(w_group[li])
        wr = wr.at[:, N_GROUPS:N_GROUPS + N_EXPERTS].set(w_router[li])
        br = jnp.zeros((1, LANES), F32).at[0, :N_GROUPS].set(b_group[li])
        br = br.at[0, N_GROUPS:N_GROUPS + N_EXPERTS].set(b_router[li])
        merge_wts = (w[:, o_g:].astype(BF16), w_conv_out[li].astype(BF16), b_conv_out[li].reshape(1, d),
                     w_attn_out[li].astype(BF16), w_out[li].astype(BF16), wr.astype(BF16), br)
        x, h2, cmb = _merge(x, conv_act, attn, (scale1, shift1, gate1, scale2, shift2), g1, g2,
                            merge_wts, tm)
        x = _moe(h2, cmb, x, gate2, w_gate[li].astype(BF16), w_up[li].astype(BF16),
                 w_down[li].astype(BF16), tmoe)
    return x


def kernel(x, c, w_ada, b_ada, g_norm1, w_in, q_norm_g, k_norm_g, conv_w, conv_b, conv_ln_g, conv_ln_b, w_conv_out, b_conv_out, w_attn_out, w_out, g_norm2, w_group, b_group, w_router, b_router, w_gate, w_up, w_down):
    return _forward(x, c, w_ada, b_ada, g_norm1, w_in, q_norm_g, k_norm_g, conv_w, conv_b,
                    conv_ln_g, conv_ln_b, w_conv_out, b_conv_out, w_attn_out, w_out, g_norm2,
                    w_group, b_group, w_router, b_router, w_gate, w_up, w_down,
                    tm=512, tl=512, tq=512, sk=512, tmoe=1024)
```

```python
import functools

import numpy as np
import jax
import jax.numpy as jnp
from jax import lax
from jax.experimental import pallas as pl
from jax.experimental.pallas import tpu as pltpu

F32 = jnp.float32
BF16 = jnp.bfloat16
I32 = jnp.int32

N_MOD = 6
CONV_CH = 512
CONV_WIDTH = 31
N_HEADS = 8
HEAD_DIM = 64
ATTN_W = N_HEADS * HEAD_DIM
IDX_HEADS = 8
IDX_DIM = 32
TOPK_MAX = 256
ROPE_THETA = 10000.0
INDEX_SCALE = (IDX_DIM ** -0.5) * (IDX_HEADS ** -0.5)
N_GROUPS = 4
EXPERTS_PER_GROUP = 8
N_EXPERTS = N_GROUPS * EXPERTS_PER_GROUP
EPS = 1e-6

LANES = 128
SUBLANES = 8
VMEM_LIMIT = 58 * 1024 * 1024
IDX_SUB = 256
SWEEP_ROWS = 128
CHEAP_ITERS = 20
Q_SCALE = (HEAD_DIM ** -0.5) * 1.4426950408889634
GROUP_LANE = N_GROUPS + N_EXPERTS
MOE_SUB = 288
MOE_EXPERTS_PER_STEP = 4
MOE_CHUNK = 256
HALO = 32


def _cparams(sem):
    return pltpu.CompilerParams(dimension_semantics=sem, vmem_limit_bytes=VMEM_LIMIT)


def _rms_mod(x, g, scale, shift):
    ms = jnp.mean(x * x, axis=-1, keepdims=True)
    y = x * lax.rsqrt(ms + EPS) * g
    return y * (1.0 + scale) + shift


def _rot_half(x, half):
    n = x.shape[-1]
    lane = lax.broadcasted_iota(I32, x.shape, x.ndim - 1)
    first = (lane % (2 * half)) < half
    return jnp.where(first, pltpu.roll(x, n - half, x.ndim - 1), pltpu.roll(x, half, x.ndim - 1))


def _mod_kernel(c_ref, w_ref, b_ref, o_ref):
    c = c_ref[...]
    sc = (c * jax.nn.sigmoid(c)).astype(BF16)
    o_ref[...] = jnp.dot(sc, w_ref[...].astype(BF16), preferred_element_type=F32) + b_ref[...]


def _modulation(c, w_ada, b_ada):
    depth, d, n = w_ada.shape
    b = c.shape[0]
    rows = 8
    c_pad = jnp.zeros((rows, d), F32).at[:b].set(c)
    tn = 1536
    out = pl.pallas_call(
        _mod_kernel,
        grid=(depth, n // tn),
        in_specs=[
            pl.BlockSpec((rows, d), lambda l, j: (0, 0)),
            pl.BlockSpec((None, d, tn), lambda l, j: (l, 0, j)),
            pl.BlockSpec((None, 1, tn), lambda l, j: (l, 0, j)),
        ],
        out_specs=pl.BlockSpec((None, rows, tn), lambda l, j: (l, 0, j)),
        out_shape=jax.ShapeDtypeStruct((depth, rows, n), F32),
        compiler_params=_cparams(("arbitrary", "arbitrary")),
        name="adaln_mod",
    )(c_pad, w_ada, b_ada.reshape(depth, 1, n))
    return out[:, :b]


def _inproj_kernel(x_ref, sc_ref, sh_ref, g1_ref, wc_ref, wq_ref, wkv_ref, wiq_ref, wik_ref, ones_ref,
                   qg_ref, kg_ref, cq_ref, sq_ref, ckv_ref, skv_ref, ci_ref, si_ref, cik_ref, sik_ref,
                   uglu_ref, q_ref, k_ref, v_ref, iq_ref, ik_ref, iw_ref):
    hb = _rms_mod(x_ref[...], g1_ref[...], sc_ref[...], sh_ref[...]).astype(BF16)

    uc = jnp.dot(hb, wc_ref[...], preferred_element_type=F32)
    uglu_ref[...] = uc[:, :CONV_CH] * jax.nn.sigmoid(uc[:, CONV_CH:])

    uq = jnp.dot(hb, wq_ref[...], preferred_element_type=F32)
    sq = uq * uq
    sq_hi = sq.astype(BF16)
    sq_lo = (sq - sq_hi.astype(F32)).astype(BF16)
    ssq = (jnp.dot(sq_hi, ones_ref[...], preferred_element_type=F32)
           + jnp.dot(sq_lo, ones_ref[...], preferred_element_type=F32))
    qn = uq * lax.rsqrt(ssq * (1.0 / HEAD_DIM) + EPS) * qg_ref[...]
    qr = (qn * cq_ref[...] + _rot_half(qn, HEAD_DIM // 2) * sq_ref[...]) * Q_SCALE
    for h in range(N_HEADS):
        q_ref[h] = qr[:, h * HEAD_DIM:(h + 1) * HEAD_DIM].astype(BF16)

    ukv = jnp.dot(hb, wkv_ref[...], preferred_element_type=F32)
    lane = lax.broadcasted_iota(I32, ukv.shape, 1)
    is_k = lane < HEAD_DIM
    ssk = jnp.sum(jnp.where(is_k, ukv * ukv, 0.0), axis=-1, keepdims=True)
    kn = ukv * lax.rsqrt(ssk * (1.0 / HEAD_DIM) + EPS) * kg_ref[...]
    kr = kn * ckv_ref[...] + _rot_half(kn, HEAD_DIM // 2) * skv_ref[...]
    k_ref[...] = kr[:, :HEAD_DIM].astype(BF16)
    v_ext = jnp.where(is_k, pltpu.roll(ukv, HEAD_DIM, 1), jnp.where(lane == HEAD_DIM, 1.0, 0.0))
    v_ref[...] = v_ext.astype(BF16)

    uiq = jnp.dot(hb, wiq_ref[...], preferred_element_type=F32)
    iqr = uiq * ci_ref[...] + _rot_half(uiq, IDX_DIM // 2) * si_ref[...]
    for h in range(IDX_HEADS):
        iq_ref[h] = iqr[:, h * IDX_DIM:(h + 1) * IDX_DIM].astype(BF16)

    uik = jnp.dot(hb, wik_ref[...], preferred_element_type=F32)
    ikr = uik * cik_ref[...] + _rot_half(uik, IDX_DIM // 2) * sik_ref[...]
    ik_ref[...] = ikr[:, :IDX_DIM].astype(BF16)
    iw_ref[...] = ikr[:, IDX_DIM:IDX_DIM + IDX_HEADS] * INDEX_SCALE


def _in_projection(x, scale1, shift1, g1, wts, tabs, tm):
    b, l, d = x.shape
    wc, wq, wkv, wiq, wik, ones_bd, qg, kg = wts
    grid = (b, l // tm)
    full = lambda a: pl.BlockSpec(a.shape, lambda bi, i: (0,) * a.ndim)
    tok = lambda n: pl.BlockSpec((None, tm, n), lambda bi, i: (bi, i, 0))
    vec = pl.BlockSpec((None, 1, d), lambda bi, i: (bi, 0, 0))
    tab = lambda a: pl.BlockSpec((tm, a.shape[1]), lambda bi, i: (i, 0))
    head = lambda n: pl.BlockSpec((None, N_HEADS, tm, n), lambda bi, i: (bi, 0, i, 0))
    return pl.pallas_call(
        _inproj_kernel,
        grid=grid,
        in_specs=[tok(d), vec, vec, full(g1), full(wc), full(wq), full(wkv), full(wiq), full(wik),
                  full(ones_bd), full(qg), full(kg)] + [tab(t) for t in tabs],
        out_specs=[tok(CONV_CH), head(HEAD_DIM), tok(HEAD_DIM), tok(LANES), head(IDX_DIM),
                   tok(IDX_DIM), tok(IDX_HEADS)],
        out_shape=[
            jax.ShapeDtypeStruct((b, l, CONV_CH), F32),
            jax.ShapeDtypeStruct((b, N_HEADS, l, HEAD_DIM), BF16),
            jax.ShapeDtypeStruct((b, l, HEAD_DIM), BF16),
            jax.ShapeDtypeStruct((b, l, LANES), BF16),
            jax.ShapeDtypeStruct((b, IDX_HEADS, l, IDX_DIM), BF16),
            jax.ShapeDtypeStruct((b, l, IDX_DIM), BF16),
            jax.ShapeDtypeStruct((b, l, IDX_HEADS), F32),
        ],
        compiler_params=_cparams(("arbitrary", "arbitrary")),
        name="in_projection",
    )(x, scale1, shift1, g1, wc, wq, wkv, wiq, wik, ones_bd, qg, kg, *tabs)


def _conv_kernel(u_ref, halo_ref, w_ref, cb_ref, g_ref, b_ref, o_ref, buf_ref, *, tl, rows):
    i = pl.program_id(1)
    buf_ref[0, 0:HALO] = jnp.where(i > 0, halo_ref[...], 0.0)
    buf_ref[0, HALO:] = u_ref[...]
    for k in range(1, SUBLANES):
        buf_ref[k, 0:tl + HALO - SUBLANES] = buf_ref[0, k:tl + HALO - SUBLANES + k]
    first = HALO - (CONV_WIDTH - 1)
    for r in range(tl // rows):
        acc = jnp.zeros((rows, CONV_CH), F32) + cb_ref[...]
        for j in range(CONV_WIDTH):
            s = r * rows + first + j
            a = s - s % SUBLANES
            acc = acc + buf_ref[s % SUBLANES, a:a + rows, :] * w_ref[j:j + 1, :]
        mu = jnp.mean(acc, axis=-1, keepdims=True)
        yc = acc - mu
        var = jnp.mean(yc * yc, axis=-1, keepdims=True)
        yn = yc * lax.rsqrt(var + EPS) * g_ref[...] + b_ref[...]
        o_ref[r * rows:(r + 1) * rows, :] = (yn * jax.nn.sigmoid(yn)).astype(BF16)


def _conv_branch(uglu, conv_w, conv_b, ln_g, ln_b, tl):
    b, l, ch = uglu.shape
    per = tl // HALO
    w_pad = jnp.zeros((HALO, ch), F32).at[:CONV_WIDTH].set(conv_w)
    row = lambda a: a.reshape(1, ch)
    full = lambda r: pl.BlockSpec((r, ch), lambda bi, i: (0, 0))
    return pl.pallas_call(
        functools.partial(_conv_kernel, tl=tl, rows=64),
        grid=(b, l // tl),
        in_specs=[
            pl.BlockSpec((None, tl, ch), lambda bi, i: (bi, i, 0)),
            pl.BlockSpec((None, HALO, ch), lambda bi, i: (bi, jnp.maximum(i * per - 1, 0), 0)),
            full(HALO), full(1), full(1), full(1),
        ],
        out_specs=pl.BlockSpec((None, tl, ch), lambda bi, i: (bi, i, 0)),
        out_shape=jax.ShapeDtypeStruct((b, l, ch), BF16),
        scratch_shapes=[pltpu.VMEM((SUBLANES, tl + HALO, ch), F32)],
        compiler_params=_cparams(("arbitrary", "arbitrary")),
        name="conformer_conv",
    )(uglu, uglu, w_pad, row(conv_b), row(ln_g), row(ln_b))


def _dsa_kernel(q_ref, k_ref, v_ref, iq_ref, ik_ref, iw_ref, o_ref,
                sc_ref, wb_ref, cb_ref, mn_ref, mx_ref, bias_ref, s_ref, p_ref, m_ref, alpha_ref, acc_ref,
                *, tq, sk, ksel):
    i = pl.program_id(1)
    nk = ((i + 1) * tq + sk - 1) // sk
    nl = sk // LANES
    hk = sk // 2
    nlh = hk // LANES
    rows = N_HEADS * tq
    nt_dims = (((1,), (1,)), ((), ()))
    kf = float(ksel)
    inf = jnp.inf

    def lanes(c):
        return slice(c * LANES, (c + 1) * LANES)

    def head_rows(h):
        return slice(h * tq, (h + 1) * tq)

    iw = iw_ref[...]
    for h in range(IDX_HEADS):
        wb_ref[h] = jnp.broadcast_to(iw[:, h:h + 1], (tq, LANES))
    iq2 = iq_ref[...].reshape(IDX_HEADS * tq, IDX_DIM)
    qpos = i * tq + lax.broadcasted_iota(I32, (tq, LANES), 0)
    lane = lax.broadcasted_iota(I32, (tq, LANES), 1)

    mn_ref[...] = jnp.full((tq, LANES), inf, F32)
    mx_ref[...] = jnp.full((tq, LANES), -inf, F32)

    def score_body(j, carry):
        base = pl.multiple_of(j * sk, sk)
        for c2 in range(sk // IDX_SUB):
            ikc = ik_ref[pl.ds(base + c2 * IDX_SUB, IDX_SUB), :]
            r = lax.dot_general(iq2, ikc, nt_dims, preferred_element_type=F32)
            for half in range(IDX_SUB // LANES):
                c = c2 * (IDX_SUB // LANES) + half
                sc = None
                for h in range(IDX_HEADS):
                    t = jnp.maximum(r[head_rows(h), lanes(half)], 0.0) * wb_ref[h]
                    sc = t if sc is None else sc + t
                mn_ref[...] = jnp.minimum(mn_ref[...], sc)
                mx_ref[...] = jnp.maximum(mx_ref[...], sc)
                kpos = base + c * LANES + lane
                sc_ref[j, :, lanes(c)] = jnp.where(kpos <= qpos, sc, -inf)
        return carry

    lax.fori_loop(0, nk, score_body, 0)

    def sweep(cand, fn, init, reduce):
        accs = []
        for g in range(tq // SWEEP_ROWS):
            rs = slice(g * SWEEP_ROWS, (g + 1) * SWEEP_ROWS)
            cb = cand[rs]

            def body(j, acc, rs=rs, cb=cb):
                for c in range(nl):
                    acc = fn(acc, sc_ref[j, rs, lanes(c)], cb)
                return acc

            accs.append(lax.fori_loop(0, nk, body, jnp.full((SWEEP_ROWS, LANES), init, F32)))
        return jnp.concatenate([reduce(a) for a in accs], axis=0)

    ones_sq = jnp.ones((LANES, LANES), BF16)

    def row_count(acc):
        return jnp.dot(acc.astype(BF16), ones_sq, preferred_element_type=F32)

    def row_min(acc):
        return jnp.broadcast_to(jnp.min(acc, axis=1, keepdims=True), acc.shape)

    count_ge = lambda cand: sweep(cand, lambda a, s, cb: a + jnp.where(s >= cb, 1.0, 0.0), 0.0, row_count)
    count_gt = lambda cand: sweep(cand, lambda a, s, cb: a + jnp.where(s > cb, 1.0, 0.0), 0.0, row_count)
    min_ge = lambda cand: sweep(cand, lambda a, s, cb: jnp.minimum(a, jnp.where(s >= cb, s, inf)),
                                inf, row_min)

    def active(clo, tie):
        return jnp.logical_and(clo > kf, tie == 0.0)

    def any_active(clo, tie):
        return (jnp.max(jnp.where(active(clo, tie), 1.0, 0.0)) > 0.0).astype(I32)

    def cheap_body(st):
        n, _, lo, hi, clo, tie = st
        go = any_active(clo, tie)
        mid = 0.5 * lo + 0.5 * hi
        cand = jnp.where(mid > lo, mid, hi)
        c = count_ge(cand)
        ge = c >= kf
        return (n + 1, go, jnp.where(ge, cand, lo), jnp.where(ge, hi, cand), jnp.where(ge, c, clo), tie)

    def snap(st):
        lo, clo, tie, need = st
        act = active(clo, tie)
        p = min_ge(lo)
        cgt = count_gt(p)
        res = jnp.logical_and(act, cgt < kf)
        return (jnp.where(act, p, lo), clo, jnp.where(res, 1.0, tie), jnp.where(res, kf - cgt, need))

    def outer_body(st):
        _, lo, hi, clo, tie, need = st
        _, _, lo, hi, clo, tie = lax.while_loop(
            lambda s: jnp.logical_and(s[0] < CHEAP_ITERS, s[1] > 0), cheap_body,
            (jnp.int32(0), jnp.int32(1), lo, hi, clo, tie))
        lo, clo, tie, need = lax.cond(any_active(clo, tie) > 0, snap, lambda s: s, (lo, clo, tie, need))
        return any_active(clo, tie), lo, hi, clo, tie, need

    lo0 = jnp.broadcast_to(jnp.min(mn_ref[...], axis=1, keepdims=True), (tq, LANES))
    hi0 = jnp.broadcast_to(jnp.max(mx_ref[...], axis=1, keepdims=True), (tq, LANES))
    clo0 = (qpos + 1).astype(F32)
    zrep = jnp.zeros((tq, LANES), F32)
    _, thr, _, _, tie, need = lax.while_loop(
        lambda s: s[0] > 0, outer_body, (any_active(clo0, zrep), lo0, hi0, clo0, zrep, zrep))

    @pl.when(jnp.max(tie) > 0.0)
    def _():
        rr = lax.broadcasted_iota(I32, (LANES, LANES), 0)
        cc = lax.broadcasted_iota(I32, (LANES, LANES), 1)
        upper = jnp.where(rr <= cc, 1.0, 0.0).astype(BF16)

        for g in range(tq // SWEEP_ROWS):
            rs = slice(g * SWEEP_ROWS, (g + 1) * SWEEP_ROWS)
            tie_g, thr_g, need_g = tie[rs], thr[rs], need[rs]

            @pl.when(jnp.max(tie_g) > 0.0)
            def _(rs=rs, tie_g=tie_g, thr_g=thr_g, need_g=need_g):
                def tie_body(j, seen):
                    for c in range(nl):
                        s = sc_ref[j, rs, lanes(c)]
                        eqb = (jnp.where(s == thr_g, 1.0, 0.0) * tie_g).astype(BF16)
                        pref = jnp.dot(eqb, upper, preferred_element_type=F32) + seen
                        drop = eqb.astype(F32) * jnp.where(pref > need_g, 1.0, 0.0) > 0.0
                        sc_ref[j, rs, lanes(c)] = jnp.where(drop, -inf, s)
                        seen = seen + jnp.dot(eqb, ones_sq, preferred_element_type=F32)
                    return seen

                lax.fori_loop(0, nk, tie_body, jnp.zeros((SWEEP_ROWS, LANES), F32))

    cb_ref[...] = thr
    q2 = q_ref[...].reshape(rows, HEAD_DIM)
    m_ref[...] = jnp.full(m_ref.shape, -inf, F32)
    acc_ref[...] = jnp.zeros(acc_ref.shape, F32)
    p_ref[1] = jnp.zeros((rows, hk), BF16)

    def qk(start):
        return lax.dot_general(q2, k_ref[pl.ds(start, hk), :], nt_dims, preferred_element_type=F32)

    def pv(slot, start):
        return jnp.dot(p_ref[slot], v_ref[pl.ds(start, hk), :], preferred_element_type=F32)

    def softmax_half(j, slot):
        for c in range(nlh):
            sel = sc_ref[j, :, lanes(slot * nlh + c)] >= cb_ref[...]
            bias_ref[slot, :, lanes(c)] = jnp.where(sel, 0.0, -inf)
        for h in range(N_HEADS):
            mx = None
            for c in range(nlh):
                t = s_ref[slot, head_rows(h), lanes(c)] + bias_ref[slot, :, lanes(c)]
                mx = t if mx is None else jnp.maximum(mx, t)
            m_old = m_ref[h]
            m_new = jnp.maximum(m_old, jnp.max(mx, axis=1, keepdims=True))
            m_safe = jnp.where(m_new == -inf, 0.0, m_new)
            for c in range(nlh):
                t = s_ref[slot, head_rows(h), lanes(c)] + bias_ref[slot, :, lanes(c)]
                p_ref[slot, head_rows(h), lanes(c)] = jnp.exp2(t - m_safe).astype(BF16)
            alpha_ref[slot, head_rows(h), :] = jnp.exp2(m_old - m_safe)
            m_ref[h] = m_new

    def attn_body(j, carry):
        base = pl.multiple_of(j * sk, sk)
        s_ref[0] = qk(base)
        pv_prev = pv(1, pl.multiple_of(jnp.maximum(base - hk, 0), hk))
        softmax_half(j, 0)
        acc_ref[...] = (acc_ref[...] + pv_prev) * alpha_ref[0]
        s_ref[1] = qk(base + hk)
        pv_prev = pv(0, base)
        softmax_half(j, 1)
        acc_ref[...] = (acc_ref[...] + pv_prev) * alpha_ref[1]
        return carry

    lax.fori_loop(0, nk, attn_body, 0)

    acc = acc_ref[...] + pv(1, pl.multiple_of(nk * sk - hk, hk))
    out = acc[:, :HEAD_DIM] / acc[:, HEAD_DIM:HEAD_DIM + 1]
    for h in range(N_HEADS):
        o_ref[:, h * HEAD_DIM:(h + 1) * HEAD_DIM] = out[head_rows(h)].astype(BF16)


def _sparse_attention(q, k, v, iq, ik, iw, tq, sk):
    b, _, l, _ = q.shape
    ksel = min(TOPK_MAX, l // 4)
    rows = N_HEADS * tq
    hk = sk // 2
    whole = lambda n: pl.BlockSpec((None, l, n), lambda bi, i: (bi, 0, 0), pipeline_mode=pl.Buffered(1))
    head = lambda n: pl.BlockSpec((None, N_HEADS, tq, n), lambda bi, i: (bi, 0, i, 0))
    return pl.pallas_call(
        functools.partial(_dsa_kernel, tq=tq, sk=sk, ksel=ksel),
        grid=(b, l // tq),
        in_specs=[head(HEAD_DIM), whole(HEAD_DIM), whole(LANES), head(IDX_DIM), whole(IDX_DIM),
                  pl.BlockSpec((None, tq, IDX_HEADS), lambda bi, i: (bi, i, 0))],
        out_specs=pl.BlockSpec((None, tq, ATTN_W), lambda bi, i: (bi, i, 0)),
        out_shape=jax.ShapeDtypeStruct((b, l, ATTN_W), BF16),
        scratch_shapes=[
            pltpu.VMEM((l // sk, tq, sk), F32),
            pltpu.VMEM((IDX_HEADS, tq, LANES), F32),
            pltpu.VMEM((tq, LANES), F32),
            pltpu.VMEM((tq, LANES), F32),
            pltpu.VMEM((tq, LANES), F32),
            pltpu.VMEM((2, tq, hk), F32),
            pltpu.VMEM((2, rows, hk), F32),
            pltpu.VMEM((2, rows, hk), BF16),
            pltpu.VMEM((N_HEADS, tq, LANES), F32),
            pltpu.VMEM((2, rows, LANES), F32),
            pltpu.VMEM((rows, LANES), F32),
        ],
        compiler_params=_cparams(("arbitrary", "arbitrary")),
        name="sparse_attention",
    )(q, k, v, iq, ik, iw)


def _merge_kernel(x_ref, ca_ref, at_ref, sc1_ref, sh1_ref, gt1_ref, sc2_ref, sh2_ref, g1_ref, g2_ref,
                  wg_ref, wco_ref, bco_ref, wao_ref, wout_ref, wr_ref, br_ref,
                  xo_ref, h2_ref, cmb_ref):
    x = x_ref[...]
    d = x.shape[-1]
    hb = _rms_mod(x, g1_ref[...], sc1_ref[...], sh1_ref[...]).astype(BF16)
    gates = jax.nn.sigmoid(jnp.dot(hb, wg_ref[...], preferred_element_type=F32))
    y_conv = jnp.dot(ca_ref[...], wco_ref[...], preferred_element_type=F32) + bco_ref[...]
    y_attn = jnp.dot(at_ref[...], wao_ref[...], preferred_element_type=F32)
    merged = gates[:, :d] * y_conv + gates[:, d:] * y_attn
    xo = x + gt1_ref[...] * jnp.dot(merged.astype(BF16), wout_ref[...], preferred_element_type=F32)
    xo_ref[...] = xo

    h2 = _rms_mod(xo, g2_ref[...], sc2_ref[...], sh2_ref[...]).astype(BF16)
    h2_ref[...] = h2

    lg = jnp.dot(h2, wr_ref[...], preferred_element_type=F32) + br_ref[...]
    lane = lax.broadcasted_iota(I32, lg.shape, 1)
    big = jnp.int32(LANES)
    is_g = lane < N_GROUPS
    gl = jnp.where(is_g, lg, -jnp.inf)
    gmax = jnp.max(gl, axis=-1, keepdims=True)
    g_sel = jnp.min(jnp.where(gl == gmax, lane, big), axis=-1, keepdims=True)
    p_group = 1.0 / jnp.sum(jnp.where(is_g, jnp.exp(gl - gmax), 0.0), axis=-1, keepdims=True)
    eid = lane - N_GROUPS
    in_grp = jnp.logical_and(jnp.logical_and(eid >= 0, eid < N_EXPERTS),
                             (eid // EXPERTS_PER_GROUP) == g_sel)
    el = jnp.where(in_grp, lg, -jnp.inf)
    v1 = jnp.max(el, axis=-1, keepdims=True)
    i1 = jnp.min(jnp.where(el == v1, lane, big), axis=-1, keepdims=True)
    el2 = jnp.where(lane == i1, -jnp.inf, el)
    v2 = jnp.max(el2, axis=-1, keepdims=True)
    i2 = jnp.min(jnp.where(el2 == v2, lane, big), axis=-1, keepdims=True)
    e21 = jnp.exp(v2 - v1)
    p1 = 1.0 / (1.0 + e21)
    p2 = e21 / (1.0 + e21)
    cmb_ref[...] = (p_group * (jnp.where(lane == i1, p1, 0.0) + jnp.where(lane == i2, p2, 0.0))
                    + jnp.where(lane == GROUP_LANE, g_sel.astype(F32), 0.0))


def _merge(x, conv_act, attn, mods, g1, g2, wts, tm):
    b, l, d = x.shape
    full = lambda a: pl.BlockSpec(a.shape, lambda bi, i: (0,) * a.ndim)
    tok = lambda n: pl.BlockSpec((None, tm, n), lambda bi, i: (bi, i, 0))
    vec = pl.BlockSpec((None, 1, d), lambda bi, i: (bi, 0, 0))
    return pl.pallas_call(
        _merge_kernel,
        grid=(b, l // tm),
        in_specs=[tok(d), tok(CONV_CH), tok(ATTN_W)] + [vec] * 5 + [full(g1), full(g2)]
                 + [full(w) for w in wts],
        out_specs=[tok(d), tok(d), tok(LANES)],
        out_shape=[jax.ShapeDtypeStruct((b, l, d), F32),
                   jax.ShapeDtypeStruct((b, l, d), BF16),
                   jax.ShapeDtypeStruct((b, l, LANES), F32)],
        compiler_params=_cparams(("arbitrary", "arbitrary")),
        name="merge_route",
    )(x, conv_act, attn, *mods, g1, g2, *wts)


def _moe_dispatch_kernel(h2_ref, cmb_ref, ltri_ref, xs_ref, cs_ref, dest_ref, meta_ref):
    tb = h2_ref.shape[0]
    nsub_max = xs_ref.shape[0] // MOE_SUB
    cmb = cmb_ref[...]
    lane = lax.broadcasted_iota(I32, (tb, LANES), 1)
    gcol = cmb[:, GROUP_LANE:GROUP_LANE + 1]
    onehot = jnp.where(lane.astype(F32) == gcol, 1.0, 0.0)
    rank = jnp.dot(ltri_ref[...], onehot.astype(BF16), preferred_element_type=F32)
    n_row = jnp.sum(onehot, axis=0, keepdims=True)
    nsub_row = jnp.zeros_like(n_row)
    for s in range(nsub_max):
        nsub_row = nsub_row + jnp.where(n_row > float(s * MOE_SUB), 1.0, 0.0)
    rr = lax.broadcasted_iota(I32, (LANES, LANES), 0)
    cc = lax.broadcasted_iota(I32, (LANES, LANES), 1)
    before = jnp.where(rr < cc, 1.0, 0.0).astype(BF16)
    nsub8 = jnp.broadcast_to(nsub_row, (8, LANES))
    off8 = jnp.dot(nsub8.astype(BF16), before, preferred_element_type=F32) * float(MOE_SUB)
    dest = jnp.sum(onehot * (off8[0:1] + rank), axis=1, keepdims=True)
    dest_b = jnp.broadcast_to(dest, (tb, LANES))
    dest_ref[...] = dest_b
    total = jnp.sum(nsub8 * jnp.where(lax.broadcasted_iota(I32, (8, LANES), 1) < N_GROUPS, 1.0, 0.0),
                    axis=1, keepdims=True)
    row8 = lax.broadcasted_iota(I32, (8, LANES), 0)
    meta = jnp.where(row8 == 0, off8, jnp.where(row8 == 1, nsub8, jnp.broadcast_to(total, (8, LANES))))
    meta_ref[...] = meta.astype(I32)
    n_used = jnp.max(total).astype(I32)

    dest_t = [jnp.transpose(dest_b[k * LANES:(k + 1) * LANES])[0:1] for k in range(tb // LANES)]
    c1 = cmb.astype(BF16)
    r1 = cmb - c1.astype(F32)
    c2 = r1.astype(BF16)
    c3 = (r1 - c2.astype(F32)).astype(BF16)
    h2 = h2_ref[...]
    sub_row = lax.broadcasted_iota(I32, (MOE_SUB, LANES), 0).astype(F32)

    def fill(s, carry):
        r0 = pl.multiple_of(s * MOE_SUB, 16)
        want = sub_row + (s * MOE_SUB).astype(F32)
        perm = jnp.concatenate([jnp.where(jnp.broadcast_to(dt, (MOE_SUB, LANES)) == want, 1.0, 0.0)
                                for dt in dest_t], axis=1).astype(BF16)
        xs_ref[pl.ds(r0, MOE_SUB), :] = jnp.dot(perm, h2, preferred_element_type=F32).astype(BF16)
        cs_ref[pl.ds(r0, MOE_SUB), :] = (jnp.dot(perm, c1, preferred_element_type=F32)
                                         + jnp.dot(perm, c2, preferred_element_type=F32)
                                         + jnp.dot(perm, c3, preferred_element_type=F32))
        return carry

    def clear(s, carry):
        r0 = pl.multiple_of(s * MOE_SUB, 16)
        xs_ref[pl.ds(r0, MOE_SUB), :] = jnp.zeros((MOE_SUB, xs_ref.shape[1]), BF16)
        cs_ref[pl.ds(r0, MOE_SUB), :] = jnp.zeros((MOE_SUB, LANES), F32)
        return carry

    lax.fori_loop(0, n_used, fill, 0)
    lax.fori_loop(n_used, nsub_max, clear, 0)


def _moe_group_kernel(off_ref, nsub_ref, xs_ref, cs_ref, wg_ref, wu_ref, wd_ref, ys_ref, acc_ref):
    blk = pl.program_id(0)
    step = pl.program_id(1)
    per_step = wg_ref.shape[0]

    @pl.when(step == 0)
    def _():
        acc_ref[...] = jnp.zeros(acc_ref.shape, F32)

    g = (step * per_step) // EXPERTS_PER_GROUP
    first = off_ref[blk * N_GROUPS + g]
    lane = lax.broadcasted_iota(I32, (MOE_SUB, LANES), 1)

    def body(s, carry):
        r0 = pl.multiple_of(first + s * MOE_SUB, 16)
        x = xs_ref[pl.ds(r0, MOE_SUB), :]
        cmb = cs_ref[pl.ds(r0, MOE_SUB), :]
        out = None
        for k in range(per_step):
            a = jnp.dot(x, wg_ref[k], preferred_element_type=F32)
            u = jnp.dot(x, wu_ref[k], preferred_element_type=F32)
            hid = (a * jax.nn.sigmoid(a)) * u
            y = jnp.dot(hid.astype(BF16), wd_ref[k], preferred_element_type=F32)
            ce = jnp.sum(jnp.where(lane == step * per_step + k + N_GROUPS, cmb, 0.0),
                         axis=-1, keepdims=True)
            out = ce * y if out is None else out + ce * y
        acc_ref[pl.ds(r0, MOE_SUB), :] += out
        return carry

    lax.fori_loop(0, nsub_ref[blk * N_GROUPS + g], body, 0)

    @pl.when(step == pl.num_programs(1) - 1)
    def _():
        ys_ref[...] = acc_ref[...].astype(BF16)


def _moe_combine_kernel(used_ref, ys_ref, dest_ref, x_ref, gt2_ref, o_ref, acc_ref):
    blk = pl.program_id(0)
    tb = x_ref.shape[0]
    acc_ref[...] = jnp.zeros(acc_ref.shape, F32)
    dest_b = dest_ref[...]
    lane = lax.broadcasted_iota(I32, (tb, LANES), 1).astype(F32)
    n_chunks = (used_ref[blk] * MOE_SUB + MOE_CHUNK - 1) // MOE_CHUNK

    def body(c, carry):
        c0 = pl.multiple_of(c * MOE_CHUNK, MOE_CHUNK)
        base = (c * MOE_CHUNK).astype(F32)
        back = jnp.concatenate([jnp.where(dest_b == lane + (base + float(k * LANES)), 1.0, 0.0)
                                for k in range(MOE_CHUNK // LANES)], axis=1).astype(BF16)
        acc_ref[...] += jnp.dot(back, ys_ref[pl.ds(c0, MOE_CHUNK), :], preferred_element_type=F32)
        return carry

    lax.fori_loop(0, n_chunks, body, 0)
    o_ref[...] = x_ref[...] + gt2_ref[...] * acc_ref[...]


def _moe(h2, cmb, x, gate2, w_gate, w_up, w_down, tb):
    b, l, d = x.shape
    ne, _, de = w_gate.shape
    per = l // tb
    nblk = b * per
    rows = MOE_SUB * (-(-(tb + N_GROUPS * (MOE_SUB - 1)) // MOE_SUB))
    rows = -(-rows // MOE_CHUNK) * MOE_CHUNK
    ltri = jnp.asarray(np.tril(np.ones((tb, tb), np.float32), -1), BF16)
    tok = lambda n: pl.BlockSpec((None, tb, n), lambda bi, i: (bi, i, 0))
    srt = lambda n: pl.BlockSpec((None, rows, n), lambda bi, i: (bi * per + i, 0, 0))
    xs, cs, dest, meta = pl.pallas_call(
        _moe_dispatch_kernel,
        grid=(b, per),
        in_specs=[tok(d), tok(LANES), pl.BlockSpec((tb, tb), lambda bi, i: (0, 0))],
        out_specs=[srt(d), srt(LANES), tok(LANES),
                   pl.BlockSpec((None, 8, LANES), lambda bi, i: (bi * per + i, 0, 0))],
        out_shape=[jax.ShapeDtypeStruct((nblk, rows, d), BF16),
                   jax.ShapeDtypeStruct((nblk, rows, LANES), F32),
                   jax.ShapeDtypeStruct((b, l, LANES), F32),
                   jax.ShapeDtypeStruct((nblk, 8, LANES), I32)],
        compiler_params=_cparams(("arbitrary", "arbitrary")),
        name="moe_dispatch",
    )(h2, cmb, ltri)
    first = meta[:, 0, :N_GROUPS].reshape(-1)
    nsub = meta[:, 1, :N_GROUPS].reshape(-1)
    used = meta[:, 2, 0]
    blk3 = lambda n: pl.BlockSpec((None, rows, n), lambda bi, e, *_: (bi, 0, 0))
    wspec = lambda r, c: pl.BlockSpec((MOE_EXPERTS_PER_STEP, r, c), lambda bi, e, *_: (e, 0, 0))
    ys = pl.pallas_call(
        _moe_group_kernel,
        grid_spec=pltpu.PrefetchScalarGridSpec(
            num_scalar_prefetch=2,
            grid=(nblk, ne // MOE_EXPERTS_PER_STEP),
            in_specs=[blk3(d), blk3(LANES), wspec(d, de), wspec(d, de), wspec(de, d)],
            out_specs=blk3(d),
            scratch_shapes=[pltpu.VMEM((rows, d), F32)]),
        out_shape=jax.ShapeDtypeStruct((nblk, rows, d), BF16),
        compiler_params=_cparams(("arbitrary", "arbitrary")),
        name="moe_experts",
    )(first, nsub, xs, cs, w_gate, w_up, w_down)
    tok1 = lambda n: pl.BlockSpec((None, tb, n), lambda k, *_: (k // per, k % per, 0))
    return pl.pallas_call(
        _moe_combine_kernel,
        grid_spec=pltpu.PrefetchScalarGridSpec(
            num_scalar_prefetch=1,
            grid=(nblk,),
            in_specs=[pl.BlockSpec((None, rows, d), lambda k, *_: (k, 0, 0)), tok1(LANES), tok1(d),
                      pl.BlockSpec((None, 1, d), lambda k, *_: (k // per, 0, 0))],
            out_specs=tok1(d),
            scratch_shapes=[pltpu.VMEM((tb, d), F32)]),
        out_shape=jax.ShapeDtypeStruct((b, l, d), F32),
        compiler_params=_cparams(("arbitrary",)),
        name="moe_combine",
    )(used, ys, dest, x, gate2)


def _rope_tables(length):
    pos = jnp.arange(length, dtype=F32)

    def cs(dim):
        inv = ROPE_THETA ** (-jnp.arange(0, dim, 2, dtype=F32) / dim)
        ang = pos[:, None] * inv[None, :]
        c, s = jnp.cos(ang), jnp.sin(ang)
        return jnp.concatenate([c, c], -1), jnp.concatenate([-s, s], -1)

    ch, sh = cs(HEAD_DIM)
    ci, si = cs(IDX_DIM)
    pad1 = lambda a: jnp.concatenate([a, jnp.ones((length, LANES - a.shape[1]), F32)], -1)
    pad0 = lambda a: jnp.concatenate([a, jnp.zeros((length, LANES - a.shape[1]), F32)], -1)
    return (jnp.tile(ch, (1, N_HEADS)), jnp.tile(sh, (1, N_HEADS)), pad1(ch), pad0(sh),
            jnp.tile(ci, (1, IDX_HEADS)), jnp.tile(si, (1, IDX_HEADS)), pad1(ci), pad0(si))


def _forward(x, c, w_ada, b_ada, g_norm1, w_in, q_norm_g, k_norm_g, conv_w, conv_b,
             conv_ln_g, conv_ln_b, w_conv_out, b_conv_out, w_attn_out, w_out, g_norm2,
             w_group, b_group, w_router, b_router, w_gate, w_up, w_down,
             *, tm, tl, tq, sk, tmoe):
    b, l, d = x.shape
    depth = w_ada.shape[0]
    tabs = _rope_tables(l)
    ones_bd = jnp.asarray(np.kron(np.eye(N_HEADS), np.ones((HEAD_DIM, HEAD_DIM))), BF16)
    mod = _modulation(c, w_ada, b_ada)
    o_q, o_k, o_iq, o_ik = 2 * CONV_CH, 2 * CONV_CH + ATTN_W, 0, 0
    o_k = o_q + ATTN_W
    o_iq = o_k + 2 * HEAD_DIM
    o_ik = o_iq + IDX_HEADS * IDX_DIM
    o_g = o_ik + IDX_DIM + IDX_HEADS
    for li in range(depth):
        shift1, scale1, gate1, shift2, scale2, gate2 = [
            m.reshape(b, 1, d) for m in jnp.split(mod[li], N_MOD, axis=-1)]
        w = w_in[li]
        wik = jnp.zeros((d, LANES), F32).at[:, :o_g - o_ik].set(w[:, o_ik:o_g])
        in_wts = (w[:, :o_q].astype(BF16), w[:, o_q:o_k].astype(BF16), w[:, o_k:o_iq].astype(BF16),
                  w[:, o_iq:o_ik].astype(BF16), wik.astype(BF16), ones_bd,
                  jnp.tile(q_norm_g[li], N_HEADS).reshape(1, ATTN_W),
                  jnp.concatenate([k_norm_g[li], jnp.ones((LANES - HEAD_DIM,), F32)]).reshape(1, LANES))
        g1 = g_norm1[li].reshape(1, d)
        g2 = g_norm2[li].reshape(1, d)
        uglu, q, k, v, iq, ik, iw = _in_projection(x, scale1, shift1, g1, in_wts, tabs, tm)
        conv_act = _conv_branch(uglu, conv_w[li], conv_b[li], conv_ln_g[li], conv_ln_b[li], tl)
        attn = _sparse_attention(q, k, v, iq, ik, iw, tq, sk)
        wr = jnp.zeros((d, LANES), F32).at[:, :N_GROUPS].set(w_group[li])
        wr = wr.at[:, N_GROUPS:N_GROUPS + N_EXPERTS].set(w_router[li])
        br = jnp.zeros((1, LANES), F32).at[0, :N_GROUPS].set(b_group[li])
        br = br.at[0, N_GROUPS:N_GROUPS + N_EXPERTS].set(b_router[li])
        merge_wts = (w[:, o_g:].astype(BF16), w_conv_out[li].astype(BF16), b_conv_out[li].reshape(1, d),
                     w_attn_out[li].astype(BF16), w_out[li].astype(BF16), wr.astype(BF16), br)
        x, h2, cmb = _merge(x, conv_act, attn, (scale1, shift1, gate1, scale2, shift2), g1, g2,
                            merge_wts, tm)
        x = _moe(h2, cmb, x, gate2, w_gate[li].astype(BF16), w_up[li].astype(BF16),
                 w_down[li].astype(BF16), tmoe)
    return x


def kernel(x, c, w_ada, b_ada, g_norm1, w_in, q_norm_g, k_norm_g, conv_w, conv_b, conv_ln_g, conv_ln_b, w_conv_out, b_conv_out, w_attn_out, w_out, g_norm2, w_group, b_group, w_router, b_router, w_gate, w_up, w_down):
    return _forward(x, c, w_ada, b_ada, g_norm1, w_in, q_norm_g, k_norm_g, conv_w, conv_b,
                    conv_ln_g, conv_ln_b, w_conv_out, b_conv_out, w_attn_out, w_out, g_norm2,
                    w_group, b_group, w_router, b_router, w_gate, w_up, w_down,
                    tm=512, tl=512, tq=512, sk=512, tmoe=1024)
```
